```python
import jax, jax.numpy as jnp
from jax import lax
import numpy as np

D_MODEL = 2048
BATCH = 8
SEQ = 4096
DEPTH = 4

GRID_W = 64
CTX_LEN = 256
N_MIXERS = 3
N_SSD_LAYERS = (DEPTH + 2) // 3
N_FOURIER_LAYERS = (DEPTH + 1) // 3
N_POOL_LAYERS = DEPTH // 3
FFN_HIDDEN = ((8 * D_MODEL // 3 + 255) // 256) * 256
SSD_EXPAND = 2
SSD_D_INNER = SSD_EXPAND * D_MODEL
SSD_HEAD_DIM = 64
SSD_HEADS = SSD_D_INNER // SSD_HEAD_DIM
SSD_GROUPS = 8
SSD_HEADS_PER_GROUP = SSD_HEADS // SSD_GROUPS
SSD_STATE = 128
SSD_CONV_W = 5
SSD_CHUNK = 128
SSD_CONV_DIM = SSD_D_INNER + 2 * SSD_GROUPS * SSD_STATE
SSD_IN_PROJ = SSD_D_INNER + SSD_CONV_DIM + 2 * SSD_HEADS
N_FOURIER_GROUPS = 4
POOL_WINDOWS = (2, 4, 8, 16)
POOL_GROUP_DIM = D_MODEL // len(POOL_WINDOWS)
POS_BASE = 10000.0
NORM_EPS = 1e-6
SSD_NORM_EPS = 1e-5

kernel_name = 'hybrid_ssd_fourier_pool_dit_trunk'


def rmsnorm(x, w, eps=NORM_EPS):
    x32 = x.astype(jnp.float32)
    y = x32 * lax.rsqrt(jnp.mean(x32 * x32, axis=-1, keepdims=True) + eps)
    return y.astype(x.dtype) * w


def ada_norm(x, w, shift, scale):
    return rmsnorm(x, w) * (1 + scale) + shift


def grid_pos_embed(n_tokens, dtype):
    rows = n_tokens // GRID_W
    t = jnp.arange(rows * GRID_W)
    r = (t // GRID_W).astype(jnp.float32)[:, None]
    col = (t % GRID_W).astype(jnp.float32)[:, None]
    quarter = D_MODEL // 4
    omega = 1.0 / (POS_BASE ** (jnp.arange(quarter, dtype=jnp.float32) / quarter))
    emb = jnp.concatenate([jnp.sin(r * omega), jnp.cos(r * omega), jnp.sin(col * omega), jnp.cos(col * omega)], axis=-1)
    return emb.astype(dtype)


def swiglu(h, w_gate, w_up, w_down):
    return (jax.nn.silu(h @ w_gate) * (h @ w_up)) @ w_down


def dwconv_centred(u, w, b):
    pad = SSD_CONV_W // 2
    out = lax.conv_general_dilated(u, w[:, None, :], window_strides=(1,), padding=[(pad, pad)],
                                   dimension_numbers=('NWC', 'WIO', 'NWC'), feature_group_count=u.shape[-1])
    return out + b


def ssd_scan(xs, dt, a, bm, cm, h0):
    b, l = xs.shape[:2]
    nc = l // SSD_CHUNK

    def chunks(t):
        return jnp.swapaxes(t.reshape(b, nc, SSD_CHUNK, *t.shape[2:]), 0, 1)

    log_a = dt.astype(jnp.float32) * a
    xdt = xs * dt[..., None]
    tri = jnp.tril(jnp.ones((SSD_CHUNK, SSD_CHUNK), dtype=bool))[None, :, :, None, None]

    def step(state, inp):
        xc, lac, bc, cc = inp
        cum = jnp.cumsum(lac, axis=1)
        decay = jnp.exp(jnp.where(tri, cum[:, :, None] - cum[:, None, :], -jnp.inf))
        scores = jnp.einsum('blgn,bsgn->blsg', cc, bc)
        y = jnp.einsum('blsg,blsge,bsgep->blgep', scores, decay, xc)
        y = y + jnp.einsum('blgn,bgepn->blgep', cc, state) * jnp.exp(cum)[..., None]
        last = cum[:, -1]
        w_in = jnp.exp(last[:, None] - cum)
        state = state * jnp.exp(last)[..., None, None] + jnp.einsum('bsgn,bsge,bsgep->bgepn', bc, w_in, xc)
        return state, y

    state, ys = lax.scan(step, h0, (chunks(xdt), chunks(log_a), chunks(bm), chunks(cm)))
    y = jnp.swapaxes(ys, 0, 1).reshape(xs.shape)
    return y.astype(xs.dtype), state


def ssd_mixer(h, hc, in_proj, conv_w, conv_b, dt_bias, a_log, d_skip, norm_w, out_proj, with_ctx):
    G, E, P, N = SSD_GROUPS, SSD_HEADS_PER_GROUP, SSD_HEAD_DIM, SSD_STATE

    def project(u):
        b, l = u.shape[:2]
        z, xbc, dt = jnp.split(u @ in_proj, [SSD_D_INNER, SSD_D_INNER + SSD_CONV_DIM], axis=-1)
        xbc = jax.nn.silu(dwconv_centred(xbc, conv_w, conv_b))
        xs, bm, cm = jnp.split(xbc, [SSD_D_INNER, SSD_D_INNER + G * N], axis=-1)
        return (z, xs.reshape(b, l, G, E, P), bm.reshape(b, l, G, N), cm.reshape(b, l, G, N),
                dt.reshape(b, l, 2, G, E))

    def gated_out(y, z):
        b, l = y.shape[:2]
        g = (y.reshape(b, l, G, -1) * jax.nn.silu(z).reshape(b, l, G, -1)).astype(jnp.float32)
        g = g * lax.rsqrt(jnp.mean(g * g, axis=-1, keepdims=True) + SSD_NORM_EPS)
        return (g.reshape(b, l, SSD_D_INNER).astype(y.dtype) * norm_w) @ out_proj

    def rev(t, d):
        return t[:, ::-1] if d == 1 else t

    z, xs, bm, cm, dt = project(h)
    zc, xsc, bmc, cmc, dtc = project(hc)
    h0 = jnp.zeros((h.shape[0], G, E, P, N), jnp.float32)
    y_lat, y_ctx = 0, 0
    for d in range(2):
        a = -jnp.exp(a_log[d].astype(jnp.float32)).reshape(G, E)
        bias = dt_bias[d].reshape(G, E)
        skip = d_skip[d].reshape(G, E)[..., None]
        dt_l = jax.nn.softplus(dt[:, :, d] + bias)
        dt_c = jax.nn.softplus(dtc[:, :, d] + bias)
        yc_d, ctx_state = ssd_scan(rev(xsc, d), rev(dt_c, d), a, rev(bmc, d), rev(cmc, d), h0)
        yl_d, _ = ssd_scan(rev(xs, d), rev(dt_l, d), a, rev(bm, d), rev(cm, d), ctx_state)
        y_lat = y_lat + rev(yl_d, d) + skip * xs
        if with_ctx:
            y_ctx = y_ctx + rev(yc_d, d) + skip * xsc
    out_ctx = gated_out(y_ctx, zc) if with_ctx else None
    return gated_out(y_lat, z), out_ctx


def fourier_mixer(h, w_out):
    b, l, d = h.shape
    hg = h.astype(jnp.float32).reshape(b, l, N_FOURIER_GROUPS, d // N_FOURIER_GROUPS)
    f = jnp.fft.fftn(hg, axes=(1, 3), norm='ortho').real
    return f.reshape(b, l, d).astype(h.dtype) @ w_out


def pool_mixer(h, w_grp, scale):
    b, l, d = h.shape
    csum = jnp.cumsum(h.astype(jnp.float32), axis=1)
    csum = jnp.concatenate([jnp.zeros((b, 1, d), jnp.float32), csum], axis=1)
    pos = jnp.arange(l)
    groups = []
    for g, win in enumerate(POOL_WINDOWS):
        lo = jnp.clip(pos - win // 2, 0, l)
        hi = jnp.clip(pos + (win - win // 2), 0, l)
        sl = slice(g * POOL_GROUP_DIM, (g + 1) * POOL_GROUP_DIM)
        cg = csum[..., sl]
        mean = (cg[:, hi] - cg[:, lo]) / (hi - lo).astype(jnp.float32)[:, None]
        groups.append(mean.astype(h.dtype) - h[..., sl])
    pooled = jnp.stack(groups, axis=2)
    out = jnp.einsum('blgi,gio->blgo', pooled, w_grp)
    return out.reshape(b, l, d) * scale


def setup_inputs(seed: int = 0) -> dict:
    key = jax.random.key(seed)
    ks = jax.random.split(key, 24)
    f32 = jnp.float32
    D = D_MODEL

    def nrm(k, shape, scale):
        return jax.random.normal(k, shape, f32) * scale

    dt0 = jnp.exp(jax.random.uniform(ks[12], (N_SSD_LAYERS, 2, SSD_HEADS), f32, np.log(1e-3), np.log(1e-1)))
    return {
        'x': nrm(ks[0], (BATCH, SEQ, D), 1.0),
        'c': nrm(ks[1], (BATCH, D), 1.0),
        'ctx': nrm(ks[2], (BATCH, CTX_LEN, D), 1.0),
        'c_ctx': nrm(ks[3], (D,), 1.0),
        'w_mod': nrm(ks[4], (DEPTH, D, 6 * D), 0.5 * D ** -0.5),
        'b_mod': nrm(ks[5], (DEPTH, 6 * D), 0.02),
        'norm_w': 1.0 + nrm(ks[6], (DEPTH, 2, D), 0.1),
        'ffn_w_gate': nrm(ks[7], (DEPTH, D, FFN_HIDDEN), D ** -0.5),
        'ffn_w_up': nrm(ks[8], (DEPTH, D, FFN_HIDDEN), D ** -0.5),
        'ffn_w_down': nrm(ks[9], (DEPTH, FFN_HIDDEN, D), FFN_HIDDEN ** -0.5),
        'ssd_in_proj': nrm(ks[10], (N_SSD_LAYERS, D, SSD_IN_PROJ), D ** -0.5),
        'ssd_conv_w': nrm(ks[11], (N_SSD_LAYERS, SSD_CONV_W, SSD_CONV_DIM), SSD_CONV_W ** -0.5),
        'ssd_conv_b': nrm(ks[13], (N_SSD_LAYERS, SSD_CONV_DIM), 0.02),
        'ssd_dt_bias': dt0 + jnp.log(-jnp.expm1(-dt0)),
        'ssd_a_log': jnp.log(jax.random.uniform(ks[14], (N_SSD_LAYERS, 2, SSD_HEADS), f32, 1.0, 16.0)),
        'ssd_d': 1.0 + nrm(ks[15], (N_SSD_LAYERS, 2, SSD_HEADS), 0.1),
        'ssd_norm_w': 1.0 + nrm(ks[16], (N_SSD_LAYERS, SSD_D_INNER), 0.1),
        'ssd_out_proj': nrm(ks[17], (N_SSD_LAYERS, SSD_D_INNER, D), SSD_D_INNER ** -0.5),
        'fourier_w_out': nrm(ks[18], (N_FOURIER_LAYERS, D, D), D ** -0.5),
        'pool_w': nrm(ks[19], (N_POOL_LAYERS, len(POOL_WINDOWS), POOL_GROUP_DIM, POOL_GROUP_DIM), POOL_GROUP_DIM ** -0.5),
        'pool_scale': 1.0 + nrm(ks[20], (N_POOL_LAYERS, D), 0.1),
        'final_norm_w': 1.0 + nrm(ks[21], (D,), 0.1),
    }


def reference(x, c, ctx, c_ctx, w_mod, b_mod, norm_w, ffn_w_gate, ffn_w_up, ffn_w_down,
              ssd_in_proj, ssd_conv_w, ssd_conv_b, ssd_dt_bias, ssd_a_log, ssd_d, ssd_norm_w, ssd_out_proj,
              fourier_w_out, pool_w, pool_scale, final_norm_w):
    x = x + grid_pos_embed(x.shape[1], x.dtype)[None]
    for i in range(DEPTH):
        kind, j = i % N_MIXERS, i // N_MIXERS
        last = i == DEPTH - 1
        need_ctx = (not last) or kind == 0
        mod = jax.nn.silu(c) @ w_mod[i] + b_mod[i]
        sh1, sc1, g1, sh2, sc2, g2 = jnp.split(mod[:, None, :], 6, axis=-1)
        h = ada_norm(x, norm_w[i, 0], sh1, sc1)
        if need_ctx:
            mod_c = jax.nn.silu(c_ctx) @ w_mod[i] + b_mod[i]
            ch1, cs1, cg1, ch2, cs2, cg2 = jnp.split(mod_c, 6, axis=-1)
            hc = ada_norm(ctx, norm_w[i, 0], ch1, cs1)
        if kind == 0:
            y, yc = ssd_mixer(h, hc, ssd_in_proj[j], ssd_conv_w[j], ssd_conv_b[j], ssd_dt_bias[j], ssd_a_log[j],
                              ssd_d[j], ssd_norm_w[j], ssd_out_proj[j], not last)
        elif kind == 1:
            y = fourier_mixer(h, fourier_w_out[j])
            yc = None if last else fourier_mixer(hc, fourier_w_out[j])
        else:
            y = pool_mixer(h, pool_w[j], pool_scale[j])
            yc = None if last else pool_mixer(hc, pool_w[j], pool_scale[j])
        x = x + g1 * y
        x = x + g2 * swiglu(ada_norm(x, norm_w[i, 1], sh2, sc2), ffn_w_gate[i], ffn_w_up[i], ffn_w_down[i])
        if not last:
            ctx = ctx + cg1 * yc
            ctx = ctx + cg2 * swiglu(ada_norm(ctx, norm_w[i, 1], ch2, cs2), ffn_w_gate[i], ffn_w_up[i], ffn_w_down[i])
    return rmsnorm(x, final_norm_w)
```

```python
import functools
import math

import jax
import jax.numpy as jnp
from jax import lax
from jax.experimental import pallas as pl
from jax.experimental.pallas import tpu as pltpu

F32 = jnp.float32
BF16 = jnp.bfloat16

GRID_WIDTH = 64
POS_BASE = 10000.0
NORM_EPS = 1e-6
SSD_NORM_EPS = 1e-5
SSD_HEAD_DIM = 64
SSD_GROUPS = 8
SSD_STATE = 128
SSD_CHUNK = 128
N_FOURIER_GROUPS = 4
POOL_WINDOWS = (2, 4, 8, 16)
N_MIXERS = 3

V7X_LANES = 128
V7X_SUBLANES = 8
V7X_VMEM_LIMIT_BYTES = 56 * 1024 * 1024
HALO_ROWS = V7X_SUBLANES
NEG_BIG = -1e30


def _cparams(semantics):
    return pltpu.CompilerParams(dimension_semantics=semantics,
                                vmem_limit_bytes=V7X_VMEM_LIMIT_BYTES)


def _tile(n, pref):
    t = min(n, pref)
    assert n % t == 0, (n, t)
    return t


def _silu(v):
    return v / (1.0 + jnp.exp(-v))


def _ada_norm(x, nw, shift, scale):
    ms = jnp.mean(x * x, axis=-1, keepdims=True)
    y = x * lax.rsqrt(ms + NORM_EPS)
    return (y * nw) * (1.0 + scale) + shift


def _split3(v):
    hi = v.astype(BF16)
    r1 = v - hi.astype(F32)
    mid = r1.astype(BF16)
    lo = (r1 - mid.astype(F32)).astype(BF16)
    return hi, mid, lo


def _dot(a, b):
    return jnp.dot(a, b, preferred_element_type=F32)


def _mod_kernel(c_ref, w_ref, b_ref, o_ref):
    s = _silu(c_ref[...])
    o_ref[...] = jnp.dot(s, w_ref[...], preferred_element_type=F32,
                         precision=lax.Precision.HIGHEST) + b_ref[...]


def _modulation(c_all, w_mod, b_mod):
    depth, d, n = w_mod.shape
    rows = c_all.shape[0]
    tn = _tile(n, 1024)
    return pl.pallas_call(
        _mod_kernel,
        grid=(depth, n // tn),
        in_specs=[
            pl.BlockSpec((rows, d), lambda i, j: (0, 0)),
            pl.BlockSpec((None, d, tn), lambda i, j: (i, 0, j)),
            pl.BlockSpec((None, 1, tn), lambda i, j: (i, 0, j)),
        ],
        out_specs=pl.BlockSpec((None, rows, tn), lambda i, j: (i, 0, j)),
        out_shape=jax.ShapeDtypeStruct((depth, rows, n), F32),
        compiler_params=_cparams(("parallel", "parallel")),
        name="modulation",
    )(c_all, w_mod, b_mod.reshape(depth, 1, n))


def _add_pos_kernel(x_ref, p_ref, o_ref):
    o_ref[...] = x_ref[...] + p_ref[...]


def _add_pos(x, pos):
    b, l, d = x.shape
    t = _tile(l, 512)
    return pl.pallas_call(
        _add_pos_kernel,
        grid=(l // t, b),
        in_specs=[pl.BlockSpec((None, t, d), lambda i, j: (j, i, 0)),
                  pl.BlockSpec((t, d), lambda i, j: (i, 0))],
        out_specs=pl.BlockSpec((None, t, d), lambda i, j: (j, i, 0)),
        out_shape=jax.ShapeDtypeStruct(x.shape, x.dtype),
        compiler_params=_cparams(("parallel", "parallel")),
        name="add_pos",
    )(x, pos)


def _norm_mm_kernel(x_ref, nw_ref, sh_ref, sc_ref, w_ref, *rest, softplus):
    if softplus:
        b_ref, o_ref, h_ref = rest
    else:
        o_ref, h_ref = rest

    @pl.when(pl.program_id(2) == 0)
    def _():
        h_ref[...] = _ada_norm(x_ref[...], nw_ref[...], sh_ref[...], sc_ref[...]).astype(BF16)

    acc = _dot(h_ref[...], w_ref[...])
    if softplus:
        v = acc + b_ref[...]
        acc = jnp.maximum(v, 0.0) + jnp.log1p(jnp.exp(-jnp.abs(v)))
    o_ref[...] = acc.astype(o_ref.dtype)


def _norm_matmul(x, nw, shift, scale, w, bias=None, *, t_pref=1024, tn_pref=1024):
    b, l, d = x.shape
    n = w.shape[1]
    t = _tile(l, t_pref)
    tn = _tile(n, tn_pref)
    softplus = bias is not None
    in_specs = [
        pl.BlockSpec((None, t, d), lambda i, j, k: (i, j, 0)),
        pl.BlockSpec((1, d), lambda i, j, k: (0, 0)),
        pl.BlockSpec((None, 1, d), lambda i, j, k: (i, 0, 0)),
        pl.BlockSpec((None, 1, d), lambda i, j, k: (i, 0, 0)),
        pl.BlockSpec((d, tn), lambda i, j, k: (0, k)),
    ]
    args = [x, nw, shift, scale, w]
    if softplus:
        in_specs.append(pl.BlockSpec((1, tn), lambda i, j, k: (0, k)))
        args.append(bias)
    return pl.pallas_call(
        functools.partial(_norm_mm_kernel, softplus=softplus),
        grid=(b, l // t, n // tn),
        in_specs=in_specs,
        out_specs=pl.BlockSpec((None, t, tn), lambda i, j, k: (i, j, k)),
        out_shape=jax.ShapeDtypeStruct((b, l, n), F32),
        scratch_shapes=[pltpu.VMEM((t, d), BF16)],
        compiler_params=_cparams(("parallel", "parallel", "arbitrary")),
        name="norm_matmul",
    )(*args)


def _conv_kernel(cur_ref, prev_ref, next_ref, w_ref, b_ref, o_ref, ext_ref, *, t, n_tiles, width):
    l = pl.program_id(1)
    pad = width // 2
    ext_ref[0:HALO_ROWS, :] = jnp.where(l > 0, prev_ref[...], 0.0)
    ext_ref[HALO_ROWS:HALO_ROWS + t, :] = cur_ref[...]
    ext_ref[HALO_ROWS + t:, :] = jnp.where(l < n_tiles - 1, next_ref[...], 0.0)
    acc = jnp.zeros(o_ref.shape, F32) + b_ref[...]
    for k in range(width):
        acc = acc + w_ref[k:k + 1, :] * ext_ref[pl.ds(HALO_ROWS - pad + k, t), :]
    o_ref[...] = _silu(acc)


def _conv_silu(zx, conv_w, conv_b, col_offset):
    b, l, _ = zx.shape
    width, c = conv_w.shape
    t = _tile(l, 512)
    tc = _tile(c, 512)
    n_tiles = l // t
    off = col_offset // tc
    rb = t // HALO_ROWS
    last_rb = l // HALO_ROWS - 1
    return pl.pallas_call(
        functools.partial(_conv_kernel, t=t, n_tiles=n_tiles, width=width),
        grid=(b, n_tiles, c // tc),
        in_specs=[
            pl.BlockSpec((None, t, tc), lambda i, j, k: (i, j, off + k)),
            pl.BlockSpec((None, HALO_ROWS, tc), lambda i, j, k: (i, jnp.maximum(j * rb - 1, 0), off + k)),
            pl.BlockSpec((None, HALO_ROWS, tc), lambda i, j, k: (i, jnp.minimum((j + 1) * rb, last_rb), off + k)),
            pl.BlockSpec((width, tc), lambda i, j, k: (0, k)),
            pl.BlockSpec((1, tc), lambda i, j, k: (0, k)),
        ],
        out_specs=pl.BlockSpec((None, t, tc), lambda i, j, k: (i, j, k)),
        out_shape=jax.ShapeDtypeStruct((b, l, c), F32),
        scratch_shapes=[pltpu.VMEM((t + 2 * HALO_ROWS, tc), F32)],
        compiler_params=_cparams(("parallel", "parallel", "parallel")),
        name="conv_silu",
    )(zx, zx, zx, conv_w, conv_b)


def _scan_kernel(xs_ref, b_ref, c_ref, dtc_ref, dtr_ref, alr_ref, alc_ref, skip_ref, h0_ref,
                 y_ref, st_ref, *, cps, heads):
    d = pl.program_id(1)
    s = pl.program_id(3)
    q = SSD_CHUNK
    p = SSD_HEAD_DIM

    @pl.when(s == 0)
    def _():
        st_ref[...] = h0_ref[...]

    sign = 1 - 2 * d
    row = lax.broadcasted_iota(jnp.int32, (q, q), 0)
    col = lax.broadcasted_iota(jnp.int32, (q, q), 1)
    tri = (row - col) * sign >= 0
    tm = tri.astype(BF16)
    tm_t = ((col - row) * sign >= 0).astype(BF16)
    lane = lax.broadcasted_iota(jnp.int32, (q, 2 * p), 1)
    lo_half = lane < p
    a_row = -jnp.exp(alr_ref[...])
    a_col = -jnp.exp(alc_ref[...])
    skip = skip_ref[...]

    def chunk(i, carry):
        j = i + d * (cps - 1 - 2 * i)
        off = pl.multiple_of(j * q, q)
        x = xs_ref[pl.ds(off, q), :]
        bc = b_ref[pl.ds(off, q), :]
        cc = c_ref[pl.ds(off, q), :]
        dtc = dtc_ref[j]
        dtr = dtr_ref[j]
        la_c = dtc * a_row
        la_r = dtr * a_col
        c_hi, c_mid, c_lo = _split3(la_c)
        cum_c = _dot(tm, c_hi) + _dot(tm, c_mid) + _dot(tm, c_lo)
        r_hi, r_mid, r_lo = _split3(la_r)
        cum_r = _dot(r_hi, tm_t) + _dot(r_mid, tm_t) + _dot(r_lo, tm_t)
        tot_r = jnp.sum(la_r, axis=1, keepdims=True)
        f_r = jnp.exp(tot_r - cum_r) * dtr

        bt = bc.T
        bt16 = bt.astype(BF16)
        scores = _dot(cc.astype(BF16), bt16)
        scores = jnp.where(tri, scores, 0.0)
        x16 = x.astype(BF16)
        st = st_ref[...]
        y_state = _dot(cc.astype(BF16), st.astype(BF16))

        y_parts, upd_parts, ec_parts = [], [], []
        for pair in range(heads // 2):
            ms, bs, ecs = [], [], []
            for e in (2 * pair, 2 * pair + 1):
                colb = jnp.broadcast_to(cum_c[:, e:e + 1], (q, q))
                rowb = cum_r[e:e + 1, :]
                dec = jnp.exp(jnp.where(tri, colb - rowb, NEG_BIG))
                ms.append((scores * dec * dtr[e:e + 1, :]).astype(BF16))
                bs.append((bt * f_r[e:e + 1, :]).astype(BF16))
                ecs.append(jnp.exp(colb))
            xp = x16[:, pair * 2 * p:(pair + 1) * 2 * p]
            zero = jnp.zeros_like(xp)
            rhs = jnp.concatenate([jnp.where(lo_half, xp, zero), jnp.where(lo_half, zero, xp)], axis=0)
            y_parts.append(_dot(jnp.concatenate(ms, axis=1), rhs))
            upd_parts.append(_dot(jnp.concatenate(bs, axis=1), rhs))
            ec_parts.append(jnp.where(lo_half, ecs[0], ecs[1]))
        ec = jnp.concatenate(ec_parts, axis=1)
        y = jnp.concatenate(y_parts, axis=1) + y_state * ec + skip * x
        y_ref[pl.ds(off, q), :] = y
        etot = jnp.where(d == 0, ec[q - 1:q, :], ec[0:1, :])
        st_ref[...] = st * etot + jnp.concatenate(upd_parts, axis=1)
        return carry

    lax.fori_loop(0, cps, chunk, 0)


def _ssd_scan(xbc, dts5, dtr5, alog_row, alog_col, skip512, h0, d_inner):
    b, l, _ = xbc.shape
    g = SSD_GROUPS
    n = SSD_STATE
    gw = d_inner // g
    heads = gw // SSD_HEAD_DIM
    nchunks = l // SSD_CHUNK
    cps = min(nchunks, 4)
    r = cps * SSD_CHUNK
    steps = l // r
    b_off = d_inner // n
    c_off = (d_inner + g * n) // n

    def cb(si, di):
        return si + di * (steps - 1 - 2 * si)

    return pl.pallas_call(
        functools.partial(_scan_kernel, cps=cps, heads=heads),
        grid=(b, 2, g, steps),
        in_specs=[
            pl.BlockSpec((None, r, gw), lambda bi, di, gi, si: (bi, cb(si, di), gi)),
            pl.BlockSpec((None, r, n), lambda bi, di, gi, si: (bi, cb(si, di), b_off + gi)),
            pl.BlockSpec((None, r, n), lambda bi, di, gi, si: (bi, cb(si, di), c_off + gi)),
            pl.BlockSpec((None, None, cps, SSD_CHUNK, heads), lambda bi, di, gi, si: (bi, di * g + gi, cb(si, di), 0, 0)),
            pl.BlockSpec((None, None, cps, heads, SSD_CHUNK), lambda bi, di, gi, si: (bi, di * g + gi, cb(si, di), 0, 0)),
            pl.BlockSpec((None, 1, heads), lambda bi, di, gi, si: (di * g + gi, 0, 0)),
            pl.BlockSpec((None, heads, 1), lambda bi, di, gi, si: (di * g + gi, 0, 0)),
            pl.BlockSpec((None, 1, gw), lambda bi, di, gi, si: (di * g + gi, 0, 0)),
            pl.BlockSpec((None, None, None, n, gw), lambda bi, di, gi, si: (bi, di, gi, 0, 0)),
        ],
        out_specs=[
            pl.BlockSpec((None, None, r, gw), lambda bi, di, gi, si: (di, bi, cb(si, di), gi)),
            pl.BlockSpec((None, None, None, n, gw), lambda bi, di, gi, si: (bi, di, gi, 0, 0)),
        ],
        out_shape=[jax.ShapeDtypeStruct((2, b, l, d_inner), F32),
                   jax.ShapeDtypeStruct((b, 2, g, n, gw), F32)],
        compiler_params=_cparams(("parallel", "parallel", "parallel", "arbitrary")),
        name="ssd_scan",
    )(xbc, xbc, xbc, dts5, dtr5, alog_row, alog_col, skip512, h0)


def _ssd_out_kernel(yf_ref, yb_ref, z_ref, nw_ref, w_ref, x_ref, g_ref, o_ref, a_ref, *, gw):
    @pl.when(pl.program_id(2) == 0)
    def _():
        for gi in range(a_ref.shape[1] // gw):
            sl = slice(gi * gw, (gi + 1) * gw)
            v = (yf_ref[:, sl] + yb_ref[:, sl]) * _silu(z_ref[:, sl])
            ms = jnp.mean(v * v, axis=-1, keepdims=True)
            a_ref[:, sl] = (v * lax.rsqrt(ms + SSD_NORM_EPS) * nw_ref[:, sl]).astype(BF16)

    o_ref[...] = x_ref[...] + g_ref[...] * _dot(a_ref[...], w_ref[...])


def _ssd_out(y2, zx, norm_w, w_out, x, gate):
    b, l, d = x.shape
    k = w_out.shape[0]
    t = _tile(l, 256)
    tn = _tile(d, 1024)
    return pl.pallas_call(
        functools.partial(_ssd_out_kernel, gw=k // SSD_GROUPS),
        grid=(b, l // t, d // tn),
        in_specs=[
            pl.BlockSpec((None, None, t, k), lambda i, j, n: (0, i, j, 0)),
            pl.BlockSpec((None, None, t, k), lambda i, j, n: (1, i, j, 0)),
            pl.BlockSpec((None, t, k), lambda i, j, n: (i, j, 0)),
            pl.BlockSpec((1, k), lambda i, j, n: (0, 0)),
            pl.BlockSpec((k, tn), lambda i, j, n: (0, n)),
            pl.BlockSpec((None, t, tn), lambda i, j, n: (i, j, n)),
            pl.BlockSpec((None, 1, tn), lambda i, j, n: (i, 0, n)),
        ],
        out_specs=pl.BlockSpec((None, t, tn), lambda i, j, n: (i, j, n)),
        out_shape=jax.ShapeDtypeStruct(x.shape, F32),
        scratch_shapes=[pltpu.VMEM((t, k), BF16)],
        compiler_params=_cparams(("parallel", "parallel", "arbitrary")),
        name="ssd_out",
    )(y2, y2, zx, norm_w, w_out, x, gate)


def _mm_res_kernel(a_ref, w_ref, x_ref, g_ref, o_ref):
    o_ref[...] = x_ref[...] + g_ref[...] * _dot(a_ref[...], w_ref[...])


def _matmul_residual(a, w, x, gate):
    b, l, d = x.shape
    k = w.shape[0]
    t = _tile(l, 1024)
    tn = _tile(d, 1024)
    return pl.pallas_call(
        _mm_res_kernel,
        grid=(b, l // t, d // tn),
        in_specs=[
            pl.BlockSpec((None, t, k), lambda i, j, n: (i, j, 0)),
            pl.BlockSpec((k, tn), lambda i, j, n: (0, n)),
            pl.BlockSpec((None, t, tn), lambda i, j, n: (i, j, n)),
            pl.BlockSpec((None, 1, tn), lambda i, j, n: (i, 0, n)),
        ],
        out_specs=pl.BlockSpec((None, t, tn), lambda i, j, n: (i, j, n)),
        out_shape=jax.ShapeDtypeStruct(x.shape, F32),
        compiler_params=_cparams(("parallel", "parallel", "parallel")),
        name="matmul_residual",
    )(a, w, x, gate)


def _ffn_kernel(x_ref, nw_ref, sh_ref, sc_ref, g_ref, wg_ref, wu_ref, wd_ref, *rest, final):
    if final:
        fw_ref, o_ref, h_ref, acc_ref = rest
    else:
        o_ref, h_ref, acc_ref = rest
    j = pl.program_id(2)

    @pl.when(j == 0)
    def _():
        h_ref[...] = _ada_norm(x_ref[...], nw_ref[...], sh_ref[...], sc_ref[...]).astype(BF16)
        acc_ref[...] = jnp.zeros_like(acc_ref)

    h = h_ref[...]
    a = _dot(h, wg_ref[...])
    u = _dot(h, wu_ref[...])
    acc_ref[...] += _dot((_silu(a) * u).astype(BF16), wd_ref[...])

    @pl.when(j == pl.num_programs(2) - 1)
    def _():
        out = x_ref[...] + g_ref[...] * acc_ref[...]
        if final:
            ms = jnp.mean(out * out, axis=-1, keepdims=True)
            out = out * lax.rsqrt(ms + NORM_EPS) * fw_ref[...]
        o_ref[...] = out


def _ffn(x, nw, shift, scale, gate, wg, wu, wd, final_w=None):
    b, l, d = x.shape
    hdim = wg.shape[1]
    t = _tile(l, 512)
    th = _tile(hdim, 512)
    final = final_w is not None
    vec = pl.BlockSpec((None, 1, d), lambda i, j, k: (i, 0, 0))
    in_specs = [
        pl.BlockSpec((None, t, d), lambda i, j, k: (i, j, 0)),
        pl.BlockSpec((1, d), lambda i, j, k: (0, 0)),
        vec, vec, vec,
        pl.BlockSpec((d, th), lambda i, j, k: (0, k)),
        pl.BlockSpec((d, th), lambda i, j, k: (0, k)),
        pl.BlockSpec((th, d), lambda i, j, k: (k, 0)),
    ]
    args = [x, nw, shift, scale, gate, wg, wu, wd]
    if final:
        in_specs.append(pl.BlockSpec((1, d), lambda i, j, k: (0, 0)))
        args.append(final_w)
    return pl.pallas_call(
        functools.partial(_ffn_kernel, final=final),
        grid=(b, l // t, hdim // th),
        in_specs=in_specs,
        out_specs=pl.BlockSpec((None, t, d), lambda i, j, k: (i, j, 0)),
        out_shape=jax.ShapeDtypeStruct(x.shape, F32),
        scratch_shapes=[pltpu.VMEM((t, d), BF16), pltpu.VMEM((t, d), F32)],
        compiler_params=_cparams(("parallel", "parallel", "arbitrary")),
        name="ffn",
    )(*args)


def _chan_dft_kernel(x_ref, nw_ref, sh_ref, sc_ref, w_ref, o_ref, *, gw):
    h = _ada_norm(x_ref[...], nw_ref[...], sh_ref[...], sc_ref[...]).astype(BF16)
    for gi in range(h.shape[1] // gw):
        sl = slice(gi * gw, (gi + 1) * gw)
        r = _dot(h[:, sl], w_ref[...])
        o_ref[0, :, sl] = r[:, :gw].astype(BF16)
        o_ref[1, :, sl] = r[:, gw:].astype(BF16)


def _chan_dft(x, nw, shift, scale, w_cs):
    b, l, d = x.shape
    gw = w_cs.shape[0]
    t = _tile(l, 512)
    vec = pl.BlockSpec((None, 1, d), lambda i, j: (i, 0, 0))
    return pl.pallas_call(
        functools.partial(_chan_dft_kernel, gw=gw),
        grid=(b, l // t),
        in_specs=[
            pl.BlockSpec((None, t, d), lambda i, j: (i, j, 0)),
            pl.BlockSpec((1, d), lambda i, j: (0, 0)),
            vec, vec,
            pl.BlockSpec((gw, 2 * gw), lambda i, j: (0, 0)),
        ],
        out_specs=pl.BlockSpec((None, 2, t, d), lambda i, j: (i, 0, j, 0)),
        out_shape=jax.ShapeDtypeStruct((b, 2, l, d), BF16),
        compiler_params=_cparams(("parallel", "parallel")),
        name="chan_dft",
    )(x, nw, shift, scale, w_cs)


def _seq_dft_kernel(a_ref, pq_ref, o_ref, acc_ref):
    k = pl.program_id(3)

    @pl.when(k == 0)
    def _():
        acc_ref[...] = jnp.zeros_like(acc_ref)

    acc_ref[...] += _dot(a_ref[...], pq_ref[...])

    @pl.when(k == pl.num_programs(3) - 1)
    def _():
        o_ref[...] = acc_ref[...].astype(o_ref.dtype)


def _seq_dft(a_mat, pq):
    b, k2, d = pq.shape
    l = a_mat.shape[0]
    tm = _tile(l, 1024)
    tn = _tile(d, 1024)
    tk = _tile(k2, 1024)
    return pl.pallas_call(
        _seq_dft_kernel,
        grid=(b, l // tm, d // tn, k2 // tk),
        in_specs=[
            pl.BlockSpec((tm, tk), lambda i, m, n, k: (m, k)),
            pl.BlockSpec((None, tk, tn), lambda i, m, n, k: (i, k, n)),
        ],
        out_specs=pl.BlockSpec((None, tm, tn), lambda i, m, n, k: (i, m, n)),
        out_shape=jax.ShapeDtypeStruct((b, l, d), BF16),
        scratch_shapes=[pltpu.VMEM((tm, tn), F32)],
        compiler_params=_cparams(("parallel", "parallel", "parallel", "arbitrary")),
        name="seq_dft",
    )(a_mat, pq)


def _dft_tables(n, scale):
    idx = jnp.arange(n, dtype=jnp.int32)
    prod = (idx[:, None] * idx[None, :]) % n
    ang = prod.astype(F32) * (2.0 * math.pi / n)
    return jnp.cos(ang) * scale, jnp.sin(ang) * scale


def _pool_kernel(cur_ref, prev_ref, next_ref, nw_ref, sh_ref, sc_ref, g_ref, w_ref, ps_ref,
                 o_ref, ext_ref, *, t, n_tiles, seq_len, gw):
    l = pl.program_id(1)
    nw, sh, sc = nw_ref[...], sh_ref[...], sc_ref[...]
    hc = _ada_norm(cur_ref[...], nw, sh, sc)
    ext_ref[0:HALO_ROWS, :] = jnp.where(l > 0, _ada_norm(prev_ref[...], nw, sh, sc), 0.0)
    ext_ref[HALO_ROWS:HALO_ROWS + t, :] = hc
    ext_ref[HALO_ROWS + t:, :] = jnp.where(l < n_tiles - 1, _ada_norm(next_ref[...], nw, sh, sc), 0.0)
    pos = l * t + lax.broadcasted_iota(jnp.int32, (t, 1), 0)
    for gi, win in enumerate(POOL_WINDOWS):
        sl = slice(gi * gw, (gi + 1) * gw)
        half = win // 2
        tot = ext_ref[pl.ds(HALO_ROWS - half, t), sl]
        for k in range(-half + 1, win - half):
            tot = tot + ext_ref[pl.ds(HALO_ROWS + k, t), sl]
        cnt = (jnp.minimum(pos + (win - half), seq_len) - jnp.maximum(pos - half, 0)).astype(F32)
        pooled = (tot / cnt - hc[:, sl]).astype(BF16)
        out = _dot(pooled, w_ref[gi]) * ps_ref[:, sl]
        o_ref[:, sl] = cur_ref[:, sl] + g_ref[:, sl] * out


def _pool_mixer(x, nw, shift, scale, gate, w_grp, pscale):
    b, l, d = x.shape
    ng, gw, _ = w_grp.shape
    assert max(POOL_WINDOWS) // 2 <= HALO_ROWS
    t = _tile(l, 512)
    n_tiles = l // t
    rb = t // HALO_ROWS
    last_rb = l // HALO_ROWS - 1
    vec = pl.BlockSpec((None, 1, d), lambda i, j: (i, 0, 0))
    return pl.pallas_call(
        functools.partial(_pool_kernel, t=t, n_tiles=n_tiles, seq_len=l, gw=gw),
        grid=(b, n_tiles),
        in_specs=[
            pl.BlockSpec((None, t, d), lambda i, j: (i, j, 0)),
            pl.BlockSpec((None, HALO_ROWS, d), lambda i, j: (i, jnp.maximum(j * rb - 1, 0), 0)),
            pl.BlockSpec((None, HALO_ROWS, d), lambda i, j: (i, jnp.minimum((j + 1) * rb, last_rb), 0)),
            pl.BlockSpec((1, d), lambda i, j: (0, 0)),
            vec, vec, vec,
            pl.BlockSpec((ng, gw, gw), lambda i, j: (0, 0, 0)),
            pl.BlockSpec((1, d), lambda i, j: (0, 0)),
        ],
        out_specs=pl.BlockSpec((None, t, d), lambda i, j: (i, j, 0)),
        out_shape=jax.ShapeDtypeStruct(x.shape, F32),
        scratch_shapes=[pltpu.VMEM((t + 2 * HALO_ROWS, d), F32)],
        compiler_params=_cparams(("parallel", "parallel")),
        name="pool_mixer",
    )(x, x, x, nw, shift, scale, gate, w_grp, pscale)


def _pos_embed(n_tokens, d):
    rows = n_tokens // GRID_WIDTH
    t = jnp.arange(rows * GRID_WIDTH)
    r = (t // GRID_WIDTH).astype(F32)[:, None]
    col = (t % GRID_WIDTH).astype(F32)[:, None]
    quarter = d // 4
    omega = 1.0 / (POS_BASE ** (jnp.arange(quarter, dtype=F32) / quarter))
    return jnp.concatenate([jnp.sin(r * omega), jnp.cos(r * omega),
                            jnp.sin(col * omega), jnp.cos(col * omega)], axis=-1)


def _dt_layouts(dts, heads_per_group):
    b, l, n2 = dts.shape
    ng = n2 // heads_per_group
    v = dts.reshape(b, l // SSD_CHUNK, SSD_CHUNK, ng, heads_per_group)
    return v.transpose(0, 3, 1, 2, 4), v.transpose(0, 3, 1, 4, 2)


def _ssd_layer(x, ctx, mods, mods_c, nw, p, with_ctx):
    sh1, sc1, g1 = mods[0], mods[1], mods[2]
    ch1, cs1, cg1 = mods_c[0], mods_c[1], mods_c[2]
    d_inner = p["out_proj"].shape[0]
    heads = p["dt_bias"].shape[-1]
    hpg = heads // SSD_GROUPS
    gw = d_inner // SSD_GROUPS
    w_main = p["in_proj"][:, :p["in_proj"].shape[1] - 2 * heads]
    w_dt = p["in_proj"][:, p["in_proj"].shape[1] - 2 * heads:]
    dt_bias = p["dt_bias"].reshape(1, 2 * heads)
    alog = p["a_log"].reshape(2 * SSD_GROUPS, hpg)
    alog_row = alog[:, None, :]
    alog_col = alog[:, :, None]
    skip512 = jnp.repeat(p["d"].reshape(2 * SSD_GROUPS, hpg), SSD_HEAD_DIM, axis=-1)[:, None, :]

    def project(u, shift, scale):
        zx = _norm_matmul(u, nw, shift, scale, w_main)
        dts = _norm_matmul(u, nw, shift, scale, w_dt, bias=dt_bias, tn_pref=2 * heads)
        xbc = _conv_silu(zx, p["conv_w"], p["conv_b"], d_inner)
        dtc5, dtr5 = _dt_layouts(dts, hpg)
        return zx, xbc, dtc5, dtr5

    zx_c, xbc_c, dtc_c, dtr_c = project(ctx, ch1, cs1)
    zx_l, xbc_l, dtc_l, dtr_l = project(x, sh1, sc1)
    b = x.shape[0]
    h0 = jnp.zeros((b, 2, SSD_GROUPS, SSD_STATE, gw), F32)
    y_c, st_c = _ssd_scan(xbc_c, dtc_c, dtr_c, alog_row, alog_col, skip512, h0, d_inner)
    y_l, _ = _ssd_scan(xbc_l, dtc_l, dtr_l, alog_row, alog_col, skip512, st_c, d_inner)
    x = _ssd_out(y_l, zx_l, p["norm_w"], p["out_proj"], x, g1)
    if with_ctx:
        ctx = _ssd_out(y_c, zx_c, p["norm_w"], p["out_proj"], ctx, cg1)
    return x, ctx


def _fourier_layer(x, ctx, mods, mods_c, nw, w_out, with_ctx):
    d = x.shape[-1]
    gw = d // N_FOURIER_GROUPS
    cc, sc_ = _dft_tables(gw, gw ** -0.5)
    w_cs = jnp.concatenate([cc, sc_], axis=1).astype(BF16)

    def mix(u, shift, scale, gate):
        l = u.shape[1]
        cl, sl = _dft_tables(l, l ** -0.5)
        a_mat = jnp.concatenate([cl, -sl], axis=1).astype(BF16)
        pq = _chan_dft(u, nw, shift, scale, w_cs)
        f = _seq_dft(a_mat, pq.reshape(u.shape[0], 2 * l, d))
        return _matmul_residual(f, w_out, u, gate)

    x = mix(x, mods[0], mods[1], mods[2])
    if with_ctx:
        ctx = mix(ctx, mods_c[0], mods_c[1], mods_c[2])
    return x, ctx


def kernel(x, c, ctx, c_ctx, w_mod, b_mod, norm_w, ffn_w_gate, ffn_w_up, ffn_w_down, ssd_in_proj, ssd_conv_w,
           ssd_conv_b, ssd_dt_bias, ssd_a_log, ssd_d, ssd_norm_w, ssd_out_proj, fourier_w_out, pool_w, pool_scale,
           final_norm_w):
    b, l, d = x.shape
    depth = w_mod.shape[0]
    n_ctx_rows = V7X_SUBLANES
    c_all = jnp.concatenate([c, jnp.broadcast_to(c_ctx[None, :], (n_ctx_rows, d))], axis=0)
    mod_all = _modulation(c_all, w_mod, b_mod)

    x = _add_pos(x, _pos_embed(l, d))
    for i in range(depth):
        kind, j = i % N_MIXERS, i // N_MIXERS
        last = i == depth - 1
        need_ctx = (not last) or kind == 0
        mods = [mod_all[i, :b, None, k * d:(k + 1) * d] for k in range(6)]
        mods_c = [jnp.broadcast_to(mod_all[i, b:b + 1, None, k * d:(k + 1) * d], (b, 1, d)) for k in range(6)]
        nw1 = norm_w[i, 0][None, :]
        nw2 = norm_w[i, 1][None, :]
        if kind == 0:
            p = dict(in_proj=ssd_in_proj[j].astype(BF16), conv_w=ssd_conv_w[j], conv_b=ssd_conv_b[j][None, :],
                     dt_bias=ssd_dt_bias[j], a_log=ssd_a_log[j], d=ssd_d[j], norm_w=ssd_norm_w[j][None, :],
                     out_proj=ssd_out_proj[j].astype(BF16))
            x, ctx = _ssd_layer(x, ctx, mods, mods_c, nw1, p, not last)
        elif kind == 1:
            x, ctx = _fourier_layer(x, ctx, mods, mods_c, nw1, fourier_w_out[j].astype(BF16), not last)
        else:
            pw = pool_w[j].astype(BF16)
            ps = pool_scale[j][None, :]
            x = _pool_mixer(x, nw1, mods[0], mods[1], mods[2], pw, ps)
            if not last:
                ctx = _pool_mixer(ctx, nw1, mods_c[0], mods_c[1], mods_c[2], pw, ps)
        wg, wu, wd = ffn_w_gate[i].astype(BF16), ffn_w_up[i].astype(BF16), ffn_w_down[i].astype(BF16)
        fw = final_norm_w[None, :] if last else None
        x = _ffn(x, nw2, mods[3], mods[4], mods[5], wg, wu, wd, fw)
        if not last:
            ctx = _ffn(ctx, nw2, mods_c[3], mods_c[4], mods_c[5], wg, wu, wd)
        del need_ctx
    return x
```

```python
import functools
import math

import jax
import jax.numpy as jnp
from jax import lax
from jax.experimental import pallas as pl
from jax.experimental.pallas import tpu as pltpu

F32 = jnp.float32
BF16 = jnp.bfloat16

GRID_WIDTH = 64
POS_BASE = 10000.0
NORM_EPS = 1e-6
SSD_NORM_EPS = 1e-5
SSD_HEAD_DIM = 64
SSD_GROUPS = 8
SSD_STATE = 128
SSD_CHUNK = 128
N_FOURIER_GROUPS = 4
POOL_WINDOWS = (2, 4, 8, 16)
N_MIXERS = 3

V7X_LANES = 128
V7X_SUBLANES = 8
V7X_VMEM_LIMIT_BYTES = 56 * 1024 * 1024
HALO_ROWS = V7X_SUBLANES
NEG_BIG = -1e30
LOG2_E = 1.4426950408889634


def _cparams(semantics):
    return pltpu.CompilerParams(dimension_semantics=semantics,
                                vmem_limit_bytes=V7X_VMEM_LIMIT_BYTES)


def _tile(n, pref):
    t = min(n, pref)
    assert n % t == 0, (n, t)
    return t


def _silu(v):
    return v / (1.0 + jnp.exp(-v))


def _ada_norm(x, nw, shift, scale):
    ms = jnp.mean(x * x, axis=-1, keepdims=True)
    y = x * lax.rsqrt(ms + NORM_EPS)
    return (y * nw) * (1.0 + scale) + shift


def _split3(v):
    hi = v.astype(BF16)
    r1 = v - hi.astype(F32)
    mid = r1.astype(BF16)
    lo = (r1 - mid.astype(F32)).astype(BF16)
    return hi, mid, lo


def _dot(a, b):
    return jnp.dot(a, b, preferred_element_type=F32)


def _mod_kernel(c_ref, w_ref, b_ref, o_ref):
    s = _silu(c_ref[...])
    o_ref[...] = jnp.dot(s, w_ref[...], preferred_element_type=F32,
                         precision=lax.Precision.HIGHEST) + b_ref[...]


def _modulation(c_all, w_mod, b_mod):
    depth, d, n = w_mod.shape
    rows = c_all.shape[0]
    tn = _tile(n, 1024)
    return pl.pallas_call(
        _mod_kernel,
        grid=(depth, n // tn),
        in_specs=[
            pl.BlockSpec((rows, d), lambda i, j: (0, 0)),
            pl.BlockSpec((None, d, tn), lambda i, j: (i, 0, j)),
            pl.BlockSpec((None, 1, tn), lambda i, j: (i, 0, j)),
        ],
        out_specs=pl.BlockSpec((None, rows, tn), lambda i, j: (i, 0, j)),
        out_shape=jax.ShapeDtypeStruct((depth, rows, n), F32),
        compiler_params=_cparams(("parallel", "parallel")),
        name="modulation",
    )(c_all, w_mod, b_mod.reshape(depth, 1, n))


def _add_pos_kernel(x_ref, p_ref, o_ref):
    o_ref[...] = x_ref[...] + p_ref[...]


def _add_pos(x, pos):
    b, l, d = x.shape
    t = _tile(l, 512)
    return pl.pallas_call(
        _add_pos_kernel,
        grid=(l // t, b),
        in_specs=[pl.BlockSpec((None, t, d), lambda i, j: (j, i, 0)),
                  pl.BlockSpec((t, d), lambda i, j: (i, 0))],
        out_specs=pl.BlockSpec((None, t, d), lambda i, j: (j, i, 0)),
        out_shape=jax.ShapeDtypeStruct(x.shape, x.dtype),
        compiler_params=_cparams(("parallel", "parallel")),
        name="add_pos",
    )(x, pos)


def _norm_mm_kernel(x_ref, nw_ref, sh_ref, sc_ref, w_ref, *rest, softplus):
    if softplus:
        b_ref, o_ref, h_ref = rest
    else:
        o_ref, h_ref = rest

    @pl.when(pl.program_id(2) == 0)
    def _():
        h_ref[...] = _ada_norm(x_ref[...], nw_ref[...], sh_ref[...], sc_ref[...]).astype(BF16)

    acc = _dot(h_ref[...], w_ref[...])
    if softplus:
        v = acc + b_ref[...]
        acc = jnp.maximum(v, 0.0) + jnp.log1p(jnp.exp(-jnp.abs(v)))
    o_ref[...] = acc.astype(o_ref.dtype)


def _norm_matmul(x, nw, shift, scale, w, bias=None, *, out_dtype=F32, t_pref=1024, tn_pref=1024):
    b, l, d = x.shape
    n = w.shape[1]
    t = _tile(l, t_pref)
    tn = _tile(n, tn_pref)
    softplus = bias is not None
    in_specs = [
        pl.BlockSpec((None, t, d), lambda i, j, k: (i, j, 0)),
        pl.BlockSpec((1, d), lambda i, j, k: (0, 0)),
        pl.BlockSpec((None, 1, d), lambda i, j, k: (i, 0, 0)),
        pl.BlockSpec((None, 1, d), lambda i, j, k: (i, 0, 0)),
        pl.BlockSpec((d, tn), lambda i, j, k: (0, k)),
    ]
    args = [x, nw, shift, scale, w]
    if softplus:
        in_specs.append(pl.BlockSpec((1, tn), lambda i, j, k: (0, k)))
        args.append(bias)
    return pl.pallas_call(
        functools.partial(_norm_mm_kernel, softplus=softplus),
        grid=(b, l // t, n // tn),
        in_specs=in_specs,
        out_specs=pl.BlockSpec((None, t, tn), lambda i, j, k: (i, j, k)),
        out_shape=jax.ShapeDtypeStruct((b, l, n), out_dtype),
        scratch_shapes=[pltpu.VMEM((t, d), BF16)],
        compiler_params=_cparams(("parallel", "parallel", "arbitrary")),
        name="norm_matmul",
    )(*args)


def _conv_kernel(cur_ref, prev_ref, next_ref, w_ref, b_ref, o_ref, ext_ref, *, t, n_tiles, width):
    l = pl.program_id(1)
    pad = width // 2
    ext_ref[0:pad] = jnp.where(l > 0, prev_ref[...].astype(F32), 0.0)
    ext_ref[pad:pad + t] = cur_ref[...].astype(F32)
    ext_ref[pad + t:] = jnp.where(l < n_tiles - 1, next_ref[...].astype(F32), 0.0)
    acc = b_ref[...][None] + w_ref[0][None] * ext_ref[pl.ds(0, t)]
    for k in range(1, width):
        acc = acc + w_ref[k][None] * ext_ref[pl.ds(k, t)]
    o_ref[...] = _silu(acc).astype(o_ref.dtype)


def _conv_silu(zx, conv_w, conv_b, col_offset):
    b, l, n_all = zx.shape
    width, c = conv_w.shape
    pad = width // 2
    sub = 2 * V7X_SUBLANES
    cb = sub * V7X_LANES
    t = _tile(l, 256)
    n_tiles = l // t
    off = col_offset // cb
    hb = t // pad
    last_hb = l // pad - 1
    zx4 = zx.reshape(b, l, n_all // V7X_LANES, V7X_LANES)
    out = pl.pallas_call(
        functools.partial(_conv_kernel, t=t, n_tiles=n_tiles, width=width),
        grid=(b, n_tiles, c // cb),
        in_specs=[
            pl.BlockSpec((None, t, sub, V7X_LANES), lambda i, j, k: (i, j, off + k, 0)),
            pl.BlockSpec((None, pad, sub, V7X_LANES), lambda i, j, k: (i, jnp.maximum(j * hb - 1, 0), off + k, 0)),
            pl.BlockSpec((None, pad, sub, V7X_LANES),
                         lambda i, j, k: (i, jnp.minimum((j + 1) * hb, last_hb), off + k, 0)),
            pl.BlockSpec((width, sub, V7X_LANES), lambda i, j, k: (0, k, 0)),
            pl.BlockSpec((sub, V7X_LANES), lambda i, j, k: (k, 0)),
        ],
        out_specs=pl.BlockSpec((None, t, sub, V7X_LANES), lambda i, j, k: (i, j, k, 0)),
        out_shape=jax.ShapeDtypeStruct((b, l, c // V7X_LANES, V7X_LANES), BF16),
        scratch_shapes=[pltpu.VMEM((t + 2 * pad, sub, V7X_LANES), F32)],
        compiler_params=_cparams(("parallel", "parallel", "parallel")),
        name="conv_silu",
    )(zx4, zx4, zx4, conv_w.reshape(width, c // V7X_LANES, V7X_LANES), conv_b.reshape(c // V7X_LANES, V7X_LANES))
    return out.reshape(b, l, c)


def _scan_kernel(xs_ref, b_ref, c_ref, dtc_ref, dtr_ref, alr_ref, alc_ref, skip_ref, h0_ref,
                 tm_ref, tmt_ref, madd_ref, y_ref, st_ref, *, cps, heads):
    d = pl.program_id(1)
    s = pl.program_id(3)
    q = SSD_CHUNK
    p = SSD_HEAD_DIM

    @pl.when(s == 0)
    def _():
        st_ref[...] = h0_ref[...]

    lane = lax.broadcasted_iota(jnp.int32, (q, 2 * p), 1)
    lo_half = lane < p
    a_row = -jnp.exp(alr_ref[...]) * LOG2_E
    a_col = -jnp.exp(alc_ref[...]) * LOG2_E
    skip = skip_ref[...]

    def chunk(i, carry):
        j = i + d * (cps - 1 - 2 * i)
        off = pl.multiple_of(j * q, q)
        x16 = xs_ref[pl.ds(off, q), :]
        b16 = b_ref[pl.ds(off, q), :]
        c16 = c_ref[pl.ds(off, q), :]
        dtc = dtc_ref[j]
        dtr = dtr_ref[j]
        tm = tm_ref[...]
        tm_t = tmt_ref[...]
        la_c = dtc * a_row
        la_r = dtr * a_col
        c_hi, c_mid, c_lo = _split3(la_c)
        cum_c = _dot(tm, c_hi) + _dot(tm, c_mid) + _dot(tm, c_lo)
        r_hi, r_mid, r_lo = _split3(la_r)
        cum_r = _dot(r_hi, tm_t) + _dot(r_mid, tm_t) + _dot(r_lo, tm_t)
        tot_r = jnp.sum(la_r, axis=1, keepdims=True)
        ldt_r = jnp.log2(dtr)
        f_r = jnp.exp2(tot_r - cum_r + ldt_r)
        rowb_all = cum_r - ldt_r

        bt = b16.astype(F32).T
        scores = _dot(c16, bt.astype(BF16))
        st = st_ref[...]
        y_state = _dot(c16, st.astype(BF16))

        y_parts, upd_parts, ec_parts = [], [], []
        for pair in range(heads // 2):
            ms, bs, ecs = [], [], []
            for e in (2 * pair, 2 * pair + 1):
                colb = jnp.broadcast_to(cum_c[:, e:e + 1], (q, q))
                dec = jnp.exp2((colb - rowb_all[e:e + 1, :]) + madd_ref[...])
                ms.append((scores * dec).astype(BF16))
                bs.append((bt * f_r[e:e + 1, :]).astype(BF16))
                ecs.append(jnp.exp2(colb))
            xp = x16[:, pair * 2 * p:(pair + 1) * 2 * p]
            zero = jnp.zeros_like(xp)
            rhs = jnp.concatenate([jnp.where(lo_half, xp, zero), jnp.where(lo_half, zero, xp)], axis=0)
            y_parts.append(_dot(jnp.concatenate(ms, axis=1), rhs))
            upd_parts.append(_dot(jnp.concatenate(bs, axis=1), rhs))
            ec_parts.append(jnp.where(lo_half, ecs[0], ecs[1]))
        ec = jnp.concatenate(ec_parts, axis=1)
        y = jnp.concatenate(y_parts, axis=1) + y_state * ec + skip * x16.astype(F32)
        y_ref[pl.ds(off, q), :] = y.astype(y_ref.dtype)
        etot = jnp.where(d == 0, ec[q - 1:q, :], ec[0:1, :])
        st_ref[...] = st * etot + jnp.concatenate(upd_parts, axis=1)
        return carry

    lax.fori_loop(0, cps, chunk, 0, unroll=True)


def _scan_masks():
    q = SSD_CHUNK
    row = jnp.arange(q)[:, None]
    col = jnp.arange(q)[None, :]
    tri = jnp.stack([row >= col, row <= col])
    return (tri.astype(BF16), jnp.swapaxes(tri, 1, 2).astype(BF16),
            jnp.where(tri, 0.0, NEG_BIG).astype(F32))


def _ssd_scan(xbc, dts5, dtr5, alog_row, alog_col, skip512, h0, d_inner):
    b, l, _ = xbc.shape
    g = SSD_GROUPS
    n = SSD_STATE
    gw = d_inner // g
    heads = gw // SSD_HEAD_DIM
    nchunks = l // SSD_CHUNK
    cps = min(nchunks, 4)
    r = cps * SSD_CHUNK
    steps = l // r
    b_off = d_inner // n
    c_off = (d_inner + g * n) // n

    def cb(si, di):
        return si + di * (steps - 1 - 2 * si)

    tm, tm_t, madd = _scan_masks()
    mask_spec = pl.BlockSpec((None, SSD_CHUNK, SSD_CHUNK), lambda bi, di, gi, si: (di, 0, 0))
    return pl.pallas_call(
        functools.partial(_scan_kernel, cps=cps, heads=heads),
        grid=(b, 2, g, steps),
        in_specs=[
            pl.BlockSpec((None, r, gw), lambda bi, di, gi, si: (bi, cb(si, di), gi)),
            pl.BlockSpec((None, r, n), lambda bi, di, gi, si: (bi, cb(si, di), b_off + gi)),
            pl.BlockSpec((None, r, n), lambda bi, di, gi, si: (bi, cb(si, di), c_off + gi)),
            pl.BlockSpec((None, None, cps, SSD_CHUNK, heads), lambda bi, di, gi, si: (bi, di * g + gi, cb(si, di), 0, 0)),
            pl.BlockSpec((None, None, cps, heads, SSD_CHUNK), lambda bi, di, gi, si: (bi, di * g + gi, cb(si, di), 0, 0)),
            pl.BlockSpec((None, 1, heads), lambda bi, di, gi, si: (di * g + gi, 0, 0)),
            pl.BlockSpec((None, heads, 1), lambda bi, di, gi, si: (di * g + gi, 0, 0)),
            pl.BlockSpec((None, 1, gw), lambda bi, di, gi, si: (di * g + gi, 0, 0)),
            pl.BlockSpec((None, None, None, n, gw), lambda bi, di, gi, si: (bi, di, gi, 0, 0)),
            mask_spec, mask_spec, mask_spec,
        ],
        out_specs=[
            pl.BlockSpec((None, None, r, gw), lambda bi, di, gi, si: (di, bi, cb(si, di), gi)),
            pl.BlockSpec((None, None, None, n, gw), lambda bi, di, gi, si: (bi, di, gi, 0, 0)),
        ],
        out_shape=[jax.ShapeDtypeStruct((2, b, l, d_inner), BF16),
                   jax.ShapeDtypeStruct((b, 2, g, n, gw), F32)],
        compiler_params=_cparams(("parallel", "parallel", "parallel", "arbitrary")),
        name="ssd_scan",
    )(xbc, xbc, xbc, dts5, dtr5, alog_row, alog_col, skip512, h0, tm, tm_t, madd)


def _ssd_out_kernel(yf_ref, yb_ref, z_ref, nw_ref, w_ref, x_ref, g_ref, o_ref, a_ref, *, gw):
    for gi in range(a_ref.shape[1] // gw):
        sl = slice(gi * gw, (gi + 1) * gw)
        v = (yf_ref[:, sl].astype(F32) + yb_ref[:, sl].astype(F32)) * _silu(z_ref[:, sl].astype(F32))
        ms = jnp.mean(v * v, axis=-1, keepdims=True)
        a_ref[:, sl] = (v * lax.rsqrt(ms + SSD_NORM_EPS) * nw_ref[:, sl]).astype(BF16)
    o_ref[...] = x_ref[...] + g_ref[...] * _dot(a_ref[...], w_ref[...])


def _ssd_out(y2, zx, norm_w, w_out, x, gate):
    b, l, d = x.shape
    k = w_out.shape[0]
    t = _tile(l, 256)
    return pl.pallas_call(
        functools.partial(_ssd_out_kernel, gw=k // SSD_GROUPS),
        grid=(b, l // t),
        in_specs=[
            pl.BlockSpec((None, None, t, k), lambda i, j: (0, i, j, 0)),
            pl.BlockSpec((None, None, t, k), lambda i, j: (1, i, j, 0)),
            pl.BlockSpec((None, t, k), lambda i, j: (i, j, 0)),
            pl.BlockSpec((1, k), lambda i, j: (0, 0)),
            pl.BlockSpec((k, d), lambda i, j: (0, 0), pipeline_mode=pl.Buffered(1)),
            pl.BlockSpec((None, t, d), lambda i, j: (i, j, 0)),
            pl.BlockSpec((None, 1, d), lambda i, j: (i, 0, 0)),
        ],
        out_specs=pl.BlockSpec((None, t, d), lambda i, j: (i, j, 0)),
        out_shape=jax.ShapeDtypeStruct(x.shape, F32),
        scratch_shapes=[pltpu.VMEM((t, k), BF16)],
        compiler_params=_cparams(("parallel", "parallel")),
        name="ssd_out",
    )(y2, y2, zx, norm_w, w_out, x, gate)


def _mm_res_kernel(a_ref, w_ref, x_ref, g_ref, o_ref):
    o_ref[...] = x_ref[...] + g_ref[...] * _dot(a_ref[...], w_ref[...])


def _matmul_residual(a, w, x, gate):
    b, l, d = x.shape
    k = w.shape[0]
    t = _tile(l, 1024)
    tn = _tile(d, 1024)
    return pl.pallas_call(
        _mm_res_kernel,
        grid=(b, l // t, d // tn),
        in_specs=[
            pl.BlockSpec((None, t, k), lambda i, j, n: (i, j, 0)),
            pl.BlockSpec((k, tn), lambda i, j, n: (0, n)),
            pl.BlockSpec((None, t, tn), lambda i, j, n: (i, j, n)),
            pl.BlockSpec((None, 1, tn), lambda i, j, n: (i, 0, n)),
        ],
        out_specs=pl.BlockSpec((None, t, tn), lambda i, j, n: (i, j, n)),
        out_shape=jax.ShapeDtypeStruct(x.shape, F32),
        compiler_params=_cparams(("parallel", "parallel", "parallel")),
        name="matmul_residual",
    )(a, w, x, gate)


def _ffn_kernel(x_ref, nw_ref, sh_ref, sc_ref, g_ref, wg_ref, wu_ref, wd_ref, *rest, final):
    if final:
        fw_ref, o_ref, h_ref, acc_ref = rest
    else:
        o_ref, h_ref, acc_ref = rest
    j = pl.program_id(2)

    @pl.when(j == 0)
    def _():
        h_ref[...] = _ada_norm(x_ref[...], nw_ref[...], sh_ref[...], sc_ref[...]).astype(BF16)
        acc_ref[...] = jnp.zeros_like(acc_ref)

    h = h_ref[...]
    a = _dot(h, wg_ref[...])
    u = _dot(h, wu_ref[...])
    acc_ref[...] += _dot((_silu(a) * u).astype(BF16), wd_ref[...])

    @pl.when(j == pl.num_programs(2) - 1)
    def _():
        out = x_ref[...] + g_ref[...] * acc_ref[...]
        if final:
            ms = jnp.mean(out * out, axis=-1, keepdims=True)
            out = out * lax.rsqrt(ms + NORM_EPS) * fw_ref[...]
        o_ref[...] = out


def _ffn(x, nw, shift, scale, gate, wg, wu, wd, final_w=None):
    b, l, d = x.shape
    hdim = wg.shape[1]
    t = _tile(l, 512)
    th = _tile(hdim, 512)
    final = final_w is not None
    vec = pl.BlockSpec((None, 1, d), lambda i, j, k: (i, 0, 0))
    in_specs = [
        pl.BlockSpec((None, t, d), lambda i, j, k: (i, j, 0)),
        pl.BlockSpec((1, d), lambda i, j, k: (0, 0)),
        vec, vec, vec,
        pl.BlockSpec((d, th), lambda i, j, k: (0, k)),
        pl.BlockSpec((d, th), lambda i, j, k: (0, k)),
        pl.BlockSpec((th, d), lambda i, j, k: (k, 0)),
    ]
    args = [x, nw, shift, scale, gate, wg, wu, wd]
    if final:
        in_specs.append(pl.BlockSpec((1, d), lambda i, j, k: (0, 0)))
        args.append(final_w)
    return pl.pallas_call(
        functools.partial(_ffn_kernel, final=final),
        grid=(b, l // t, hdim // th),
        in_specs=in_specs,
        out_specs=pl.BlockSpec((None, t, d), lambda i, j, k: (i, j, 0)),
        out_shape=jax.ShapeDtypeStruct(x.shape, F32),
        scratch_shapes=[pltpu.VMEM((t, d), BF16), pltpu.VMEM((t, d), F32)],
        compiler_params=_cparams(("parallel", "parallel", "arbitrary")),
        name="ffn",
    )(*args)


def _chan_dft_kernel(x_ref, nw_ref, sh_ref, sc_ref, w_ref, o_ref, *, gw):
    h = _ada_norm(x_ref[...], nw_ref[...], sh_ref[...], sc_ref[...]).astype(BF16)
    for gi in range(h.shape[1] // gw):
        sl = slice(gi * gw, (gi + 1) * gw)
        r = _dot(h[:, sl], w_ref[...])
        o_ref[0, :, sl] = r[:, :gw].astype(BF16)
        o_ref[1, :, sl] = r[:, gw:].astype(BF16)


def _chan_dft(x, nw, shift, scale, w_cs):
    b, l, d = x.shape
    gw = w_cs.shape[0]
    t = _tile(l, 512)
    vec = pl.BlockSpec((None, 1, d), lambda i, j: (i, 0, 0))
    return pl.pallas_call(
        functools.partial(_chan_dft_kernel, gw=gw),
        grid=(b, l // t),
        in_specs=[
            pl.BlockSpec((None, t, d), lambda i, j: (i, j, 0)),
            pl.BlockSpec((1, d), lambda i, j: (0, 0)),
            vec, vec,
            pl.BlockSpec((gw, 2 * gw), lambda i, j: (0, 0)),
        ],
        out_specs=pl.BlockSpec((None, 2, t, d), lambda i, j: (i, 0, j, 0)),
        out_shape=jax.ShapeDtypeStruct((b, 2, l, d), BF16),
        compiler_params=_cparams(("parallel", "parallel")),
        name="chan_dft",
    )(x, nw, shift, scale, w_cs)


def _seq_dft_kernel(a_ref, pq_ref, o_ref, acc_ref):
    k = pl.program_id(3)

    @pl.when(k == 0)
    def _():
        acc_ref[...] = jnp.zeros_like(acc_ref)

    acc_ref[...] += _dot(a_ref[...], pq_ref[...])

    @pl.when(k == pl.num_programs(3) - 1)
    def _():
        o_ref[...] = acc_ref[...].astype(o_ref.dtype)


def _seq_dft(a_mat, pq):
    b, k2, d = pq.shape
    l = a_mat.shape[0]
    tm = _tile(l, 1024)
    tn = _tile(d, 1024)
    tk = _tile(k2, 1024)
    return pl.pallas_call(
        _seq_dft_kernel,
        grid=(b, l // tm, d // tn, k2 // tk),
        in_specs=[
            pl.BlockSpec((tm, tk), lambda i, m, n, k: (m, k)),
            pl.BlockSpec((None, tk, tn), lambda i, m, n, k: (i, k, n)),
        ],
        out_specs=pl.BlockSpec((None, tm, tn), lambda i, m, n, k: (i, m, n)),
        out_shape=jax.ShapeDtypeStruct((b, l, d), BF16),
        scratch_shapes=[pltpu.VMEM((tm, tn), F32)],
        compiler_params=_cparams(("parallel", "parallel", "parallel", "arbitrary")),
        name="seq_dft",
    )(a_mat, pq)


def _dft_tables(n, scale):
    idx = jnp.arange(n, dtype=jnp.int32)
    prod = (idx[:, None] * idx[None, :]) % n
    ang = prod.astype(F32) * (2.0 * math.pi / n)
    return jnp.cos(ang) * scale, jnp.sin(ang) * scale


def _pool_kernel(cur_ref, prev_ref, next_ref, nw_ref, sh_ref, sc_ref, g_ref, w_ref, ps_ref,
                 o_ref, ext_ref, *, t, n_tiles, seq_len, gw):
    l = pl.program_id(1)
    nw, sh, sc = nw_ref[...], sh_ref[...], sc_ref[...]
    hc = _ada_norm(cur_ref[...], nw, sh, sc)
    ext_ref[0:HALO_ROWS, :] = jnp.where(l > 0, _ada_norm(prev_ref[...], nw, sh, sc), 0.0)
    ext_ref[HALO_ROWS:HALO_ROWS + t, :] = hc
    ext_ref[HALO_ROWS + t:, :] = jnp.where(l < n_tiles - 1, _ada_norm(next_ref[...], nw, sh, sc), 0.0)
    pos = l * t + lax.broadcasted_iota(jnp.int32, (t, 1), 0)
    for gi, win in enumerate(POOL_WINDOWS):
        sl = slice(gi * gw, (gi + 1) * gw)
        half = win // 2
        tot = ext_ref[pl.ds(HALO_ROWS - half, t), sl]
        for k in range(-half + 1, win - half):
            tot = tot + ext_ref[pl.ds(HALO_ROWS + k, t), sl]
        cnt = (jnp.minimum(pos + (win - half), seq_len) - jnp.maximum(pos - half, 0)).astype(F32)
        pooled = (tot / cnt - hc[:, sl]).astype(BF16)
        out = _dot(pooled, w_ref[gi]) * ps_ref[:, sl]
        o_ref[:, sl] = cur_ref[:, sl] + g_ref[:, sl] * out


def _pool_mixer(x, nw, shift, scale, gate, w_grp, pscale):
    b, l, d = x.shape
    ng, gw, _ = w_grp.shape
    assert max(POOL_WINDOWS) // 2 <= HALO_ROWS
    t = _tile(l, 512)
    n_tiles = l // t
    rb = t // HALO_ROWS
    last_rb = l // HALO_ROWS - 1
    vec = pl.BlockSpec((None, 1, d), lambda i, j: (i, 0, 0))
    return pl.pallas_call(
        functools.partial(_pool_kernel, t=t, n_tiles=n_tiles, seq_len=l, gw=gw),
        grid=(b, n_tiles),
        in_specs=[
            pl.BlockSpec((None, t, d), lambda i, j: (i, j, 0)),
            pl.BlockSpec((None, HALO_ROWS, d), lambda i, j: (i, jnp.maximum(j * rb - 1, 0), 0)),
            pl.BlockSpec((None, HALO_ROWS, d), lambda i, j: (i, jnp.minimum((j + 1) * rb, last_rb), 0)),
            pl.BlockSpec((1, d), lambda i, j: (0, 0)),
            vec, vec, vec,
            pl.BlockSpec((ng, gw, gw), lambda i, j: (0, 0, 0)),
            pl.BlockSpec((1, d), lambda i, j: (0, 0)),
        ],
        out_specs=pl.BlockSpec((None, t, d), lambda i, j: (i, j, 0)),
        out_shape=jax.ShapeDtypeStruct(x.shape, F32),
        scratch_shapes=[pltpu.VMEM((t + 2 * HALO_ROWS, d), F32)],
        compiler_params=_cparams(("parallel", "parallel")),
        name="pool_mixer",
    )(x, x, x, nw, shift, scale, gate, w_grp, pscale)


def _pos_embed(n_tokens, d):
    rows = n_tokens // GRID_WIDTH
    t = jnp.arange(rows * GRID_WIDTH)
    r = (t // GRID_WIDTH).astype(F32)[:, None]
    col = (t % GRID_WIDTH).astype(F32)[:, None]
    quarter = d // 4
    omega = 1.0 / (POS_BASE ** (jnp.arange(quarter, dtype=F32) / quarter))
    return jnp.concatenate([jnp.sin(r * omega), jnp.cos(r * omega),
                            jnp.sin(col * omega), jnp.cos(col * omega)], axis=-1)


def _dt_layouts(dts, heads_per_group):
    b, l, n2 = dts.shape
    ng = n2 // heads_per_group
    v = dts.reshape(b, l // SSD_CHUNK, SSD_CHUNK, ng, heads_per_group)
    return v.transpose(0, 3, 1, 2, 4), v.transpose(0, 3, 1, 4, 2)


def _ssd_layer(x, ctx, mods, mods_c, nw, p, with_ctx):
    sh1, sc1, g1 = mods[0], mods[1], mods[2]
    ch1, cs1, cg1 = mods_c[0], mods_c[1], mods_c[2]
    d_inner = p["out_proj"].shape[0]
    heads = p["dt_bias"].shape[-1]
    hpg = heads // SSD_GROUPS
    gw = d_inner // SSD_GROUPS
    w_main = p["in_proj"][:, :p["in_proj"].shape[1] - 2 * heads]
    w_dt = p["in_proj"][:, p["in_proj"].shape[1] - 2 * heads:]
    dt_bias = p["dt_bias"].reshape(1, 2 * heads)
    alog = p["a_log"].reshape(2 * SSD_GROUPS, hpg)
    alog_row = alog[:, None, :]
    alog_col = alog[:, :, None]
    skip512 = jnp.repeat(p["d"].reshape(2 * SSD_GROUPS, hpg), SSD_HEAD_DIM, axis=-1)[:, None, :]

    def project(u, shift, scale):
        zx = _norm_matmul(u, nw, shift, scale, w_main, out_dtype=BF16)
        dts = _norm_matmul(u, nw, shift, scale, w_dt, bias=dt_bias, tn_pref=2 * heads)
        xbc = _conv_silu(zx, p["conv_w"], p["conv_b"], d_inner)
        dtc5, dtr5 = _dt_layouts(dts, hpg)
        return zx, xbc, dtc5, dtr5

    zx_c, xbc_c, dtc_c, dtr_c = project(ctx, ch1, cs1)
    zx_l, xbc_l, dtc_l, dtr_l = project(x, sh1, sc1)
    b = x.shape[0]
    h0 = jnp.zeros((b, 2, SSD_GROUPS, SSD_STATE, gw), F32)
    y_c, st_c = _ssd_scan(xbc_c, dtc_c, dtr_c, alog_row, alog_col, skip512, h0, d_inner)
    y_l, _ = _ssd_scan(xbc_l, dtc_l, dtr_l, alog_row, alog_col, skip512, st_c, d_inner)
    x = _ssd_out(y_l, zx_l, p["norm_w"], p["out_proj"], x, g1)
    if with_ctx:
        ctx = _ssd_out(y_c, zx_c, p["norm_w"], p["out_proj"], ctx, cg1)
    return x, ctx


def _fourier_layer(x, ctx, mods, mods_c, nw, w_out, with_ctx):
    d = x.shape[-1]
    gw = d // N_FOURIER_GROUPS
    cc, sc_ = _dft_tables(gw, gw ** -0.5)
    w_cs = jnp.concatenate([cc, sc_], axis=1).astype(BF16)

    def mix(u, shift, scale, gate):
        l = u.shape[1]
        cl, sl = _dft_tables(l, l ** -0.5)
        a_mat = jnp.concatenate([cl, -sl], axis=1).astype(BF16)
        pq = _chan_dft(u, nw, shift, scale, w_cs)
        f = _seq_dft(a_mat, pq.reshape(u.shape[0], 2 * l, d))
        return _matmul_residual(f, w_out, u, gate)

    x = mix(x, mods[0], mods[1], mods[2])
    if with_ctx:
        ctx = mix(ctx, mods_c[0], mods_c[1], mods_c[2])
    return x, ctx


def kernel(x, c, ctx, c_ctx, w_mod, b_mod, norm_w, ffn_w_gate, ffn_w_up, ffn_w_down, ssd_in_proj, ssd_conv_w,
           ssd_conv_b, ssd_dt_bias, ssd_a_log, ssd_d, ssd_norm_w, ssd_out_proj, fourier_w_out, pool_w, pool_scale,
           final_norm_w):
    b, l, d = x.shape
    depth = w_mod.shape[0]
    n_ctx_rows = V7X_SUBLANES
    c_all = jnp.concatenate([c, jnp.broadcast_to(c_ctx[None, :], (n_ctx_rows, d))], axis=0)
    mod_all = _modulation(c_all, w_mod, b_mod)

    x = _add_pos(x, _pos_embed(l, d))
    for i in range(depth):
        kind, j = i % N_MIXERS, i // N_MIXERS
        last = i == depth - 1
        need_ctx = (not last) or kind == 0
        mods = [mod_all[i, :b, None, k * d:(k + 1) * d] for k in range(6)]
        mods_c = [jnp.broadcast_to(mod_all[i, b:b + 1, None, k * d:(k + 1) * d], (b, 1, d)) for k in range(6)]
        nw1 = norm_w[i, 0][None, :]
        nw2 = norm_w[i, 1][None, :]
        if kind == 0:
            p = dict(in_proj=ssd_in_proj[j].astype(BF16), conv_w=ssd_conv_w[j], conv_b=ssd_conv_b[j][None, :],
                     dt_bias=ssd_dt_bias[j], a_log=ssd_a_log[j], d=ssd_d[j], norm_w=ssd_norm_w[j][None, :],
                     out_proj=ssd_out_proj[j].astype(BF16))
            x, ctx = _ssd_layer(x, ctx, mods, mods_c, nw1, p, not last)
        elif kind == 1:
            x, ctx = _fourier_layer(x, ctx, mods, mods_c, nw1, fourier_w_out[j].astype(BF16), not last)
        else:
            pw = pool_w[j].astype(BF16)
            ps = pool_scale[j][None, :]
            x = _pool_mixer(x, nw1, mods[0], mods[1], mods[2], pw, ps)
            if not last:
                ctx = _pool_mixer(ctx, nw1, mods_c[0], mods_c[1], mods_c[2], pw, ps)
        wg, wu, wd = ffn_w_gate[i].astype(BF16), ffn_w_up[i].astype(BF16), ffn_w_down[i].astype(BF16)
        fw = final_norm_w[None, :] if last else None
        x = _ffn(x, nw2, mods[3], mods[4], mods[5], wg, wu, wd, fw)
        if not last:
            ctx = _ffn(ctx, nw2, mods_c[3], mods_c[4], mods_c[5], wg, wu, wd)
        del need_ctx
    return x
```

```python
import functools
import math

import jax
import jax.numpy as jnp
from jax import lax
from jax.experimental import pallas as pl
from jax.experimental.pallas import tpu as pltpu

F32 = jnp.float32
BF16 = jnp.bfloat16

GRID_WIDTH = 64
POS_BASE = 10000.0
NORM_EPS = 1e-6
SSD_NORM_EPS = 1e-5
SSD_HEAD_DIM = 64
SSD_GROUPS = 8
SSD_STATE = 128
SSD_CHUNK = 128
N_FOURIER_GROUPS = 4
POOL_WINDOWS = (2, 4, 8, 16)
N_MIXERS = 3
SCAN_CHUNKS_PER_STEP = 8

V7X_LANES = 128
V7X_SUBLANES = 8
V7X_VMEM_BYTES = 64 * 1024 * 1024
V7X_VMEM_LIMIT_BYTES = V7X_VMEM_BYTES - 8 * 1024 * 1024
HALO_ROWS = V7X_SUBLANES
NEG_BIG = -1e30
LOG2_E = 1.4426950408889634


def _cparams(semantics):
    return pltpu.CompilerParams(dimension_semantics=semantics,
                                vmem_limit_bytes=V7X_VMEM_LIMIT_BYTES)


def _tile(n, pref):
    t = min(n, pref)
    assert n % t == 0, (n, t)
    return t


def _silu(v):
    h = 0.5 * v
    return h + h * jnp.tanh(h)


def _ada_norm(x, nw, shift, scale):
    ms = jnp.mean(x * x, axis=-1, keepdims=True)
    y = x * lax.rsqrt(ms + NORM_EPS)
    return (y * nw) * (1.0 + scale) + shift


def _split3(v):
    hi = v.astype(BF16)
    r1 = v - hi.astype(F32)
    mid = r1.astype(BF16)
    lo = (r1 - mid.astype(F32)).astype(BF16)
    return hi, mid, lo


def _dot(a, b):
    return jnp.dot(a, b, preferred_element_type=F32)


def _mod_kernel(c_ref, w_ref, b_ref, o_ref):
    s = _silu(c_ref[...])
    o_ref[...] = jnp.dot(s, w_ref[...], preferred_element_type=F32,
                         precision=lax.Precision.HIGHEST) + b_ref[...]


def _modulation(c_all, w_mod, b_mod):
    depth, d, n = w_mod.shape
    rows = c_all.shape[0]
    tn = _tile(n, 1024)
    return pl.pallas_call(
        _mod_kernel,
        grid=(depth, n // tn),
        in_specs=[
            pl.BlockSpec((rows, d), lambda i, j: (0, 0)),
            pl.BlockSpec((None, d, tn), lambda i, j: (i, 0, j)),
            pl.BlockSpec((None, 1, tn), lambda i, j: (i, 0, j)),
        ],
        out_specs=pl.BlockSpec((None, rows, tn), lambda i, j: (i, 0, j)),
        out_shape=jax.ShapeDtypeStruct((depth, rows, n), F32),
        compiler_params=_cparams(("parallel", "parallel")),
        name="modulation",
    )(c_all, w_mod, b_mod.reshape(depth, 1, n))


def _add_pos_kernel(x_ref, p_ref, o_ref):
    o_ref[...] = x_ref[...] + p_ref[...]


def _add_pos(x, pos):
    b, l, d = x.shape
    t = _tile(l, 512)
    return pl.pallas_call(
        _add_pos_kernel,
        grid=(l // t, b),
        in_specs=[pl.BlockSpec((None, t, d), lambda i, j: (j, i, 0)),
                  pl.BlockSpec((t, d), lambda i, j: (i, 0))],
        out_specs=pl.BlockSpec((None, t, d), lambda i, j: (j, i, 0)),
        out_shape=jax.ShapeDtypeStruct(x.shape, x.dtype),
        compiler_params=_cparams(("parallel", "parallel")),
        name="add_pos",
    )(x, pos)


def _in_proj_kernel(x_ref, nw_ref, sh_ref, sc_ref, w_ref, wdt_ref, b_ref, o_ref, dt_ref, h_ref):
    @pl.when(pl.program_id(2) == 0)
    def _():
        h = _ada_norm(x_ref[...], nw_ref[...], sh_ref[...], sc_ref[...]).astype(BF16)
        h_ref[...] = h
        v = _dot(h, wdt_ref[...]) + b_ref[...]
        dt_ref[...] = jnp.maximum(v, 0.0) + jnp.log1p(jnp.exp(-jnp.abs(v)))

    o_ref[...] = _dot(h_ref[...], w_ref[...]).astype(o_ref.dtype)


def _in_proj(x, nw, shift, scale, w, w_dt, dt_bias):
    b, l, d = x.shape
    n = w.shape[1]
    ndt = w_dt.shape[1]
    t = _tile(l, 1024)
    tn = _tile(n, 1024)
    return pl.pallas_call(
        _in_proj_kernel,
        grid=(b, l // t, n // tn),
        in_specs=[
            pl.BlockSpec((None, t, d), lambda i, j, k: (i, j, 0)),
            pl.BlockSpec((1, d), lambda i, j, k: (0, 0)),
            pl.BlockSpec((None, 1, d), lambda i, j, k: (i, 0, 0)),
            pl.BlockSpec((None, 1, d), lambda i, j, k: (i, 0, 0)),
            pl.BlockSpec((d, tn), lambda i, j, k: (0, k)),
            pl.BlockSpec((d, ndt), lambda i, j, k: (0, 0)),
            pl.BlockSpec((1, ndt), lambda i, j, k: (0, 0)),
        ],
        out_specs=[pl.BlockSpec((None, t, tn), lambda i, j, k: (i, j, k)),
                   pl.BlockSpec((None, t, ndt), lambda i, j, k: (i, j, 0))],
        out_shape=[jax.ShapeDtypeStruct((b, l, n), BF16), jax.ShapeDtypeStruct((b, l, ndt), F32)],
        scratch_shapes=[pltpu.VMEM((t, d), BF16)],
        compiler_params=_cparams(("parallel", "parallel", "arbitrary")),
        name="in_proj",
    )(x, nw, shift, scale, w, w_dt, dt_bias)


def _conv_kernel(cur_ref, prev_ref, next_ref, sel_ref, w_ref, b_ref, o_ref, *, t, n_tiles, width):
    l = pl.program_id(1)
    pad = width // 2
    halo = prev_ref.shape[0]
    sb = sel_ref.shape[1]
    zero = jnp.zeros(prev_ref.shape, prev_ref.dtype)
    prev = jnp.where(l > 0, prev_ref[...], zero)
    nxt = jnp.where(l < n_tiles - 1, next_ref[...], zero)
    ext = jnp.concatenate([prev, cur_ref[...], nxt], axis=0)
    for s in range(t // sb):
        win = ext[s * sb:(s + 1) * sb + 2 * halo]
        acc = b_ref[...] + w_ref[pad:pad + 1, :] * win[halo:halo + sb].astype(F32)
        for k in range(width):
            if k != pad:
                acc = acc + w_ref[k:k + 1, :] * _dot(sel_ref[k], win)
        o_ref[s * sb:(s + 1) * sb, :] = _silu(acc).astype(o_ref.dtype)


def _conv_silu(zx, conv_w, conv_b, col_offset):
    b, l, _ = zx.shape
    width, c = conv_w.shape
    pad = width // 2
    halo = 2 * V7X_SUBLANES
    assert pad <= halo
    t = _tile(l, 512)
    tc = _tile(c, 512)
    sb = _tile(t, V7X_LANES)
    n_tiles = l // t
    off = col_offset // tc
    hb = t // halo
    last_hb = l // halo - 1
    i_idx = jnp.arange(sb)[None, :, None]
    j_idx = jnp.arange(sb + 2 * halo)[None, None, :]
    k_idx = jnp.arange(width)[:, None, None]
    sel = (j_idx == i_idx + halo + k_idx - pad).astype(BF16)
    return pl.pallas_call(
        functools.partial(_conv_kernel, t=t, n_tiles=n_tiles, width=width),
        grid=(b, n_tiles, c // tc),
        in_specs=[
            pl.BlockSpec((None, t, tc), lambda i, j, k: (i, j, off + k)),
            pl.BlockSpec((None, halo, tc), lambda i, j, k: (i, jnp.maximum(j * hb - 1, 0), off + k)),
            pl.BlockSpec((None, halo, tc), lambda i, j, k: (i, jnp.minimum((j + 1) * hb, last_hb), off + k)),
            pl.BlockSpec((width, sb, sb + 2 * halo), lambda i, j, k: (0, 0, 0)),
            pl.BlockSpec((width, tc), lambda i, j, k: (0, k)),
            pl.BlockSpec((1, tc), lambda i, j, k: (0, k)),
        ],
        out_specs=pl.BlockSpec((None, t, tc), lambda i, j, k: (i, j, k)),
        out_shape=jax.ShapeDtypeStruct((b, l, c), BF16),
        compiler_params=_cparams(("parallel", "parallel", "parallel")),
        name="conv_silu",
    )(zx, zx, zx, sel, conv_w, conv_b)


def _scan_kernel(xs_ref, b_ref, c_ref, dtc_ref, dtr_ref, alr_ref, alc_ref, skip_ref, h0_ref,
                 tm_ref, tmt_ref, madd_ref, y_ref, st_ref, *, cps, heads):
    d = pl.program_id(1)
    s = pl.program_id(3)
    q = SSD_CHUNK
    p = SSD_HEAD_DIM

    @pl.when(s == 0)
    def _():
        st_ref[...] = h0_ref[...]

    lane = lax.broadcasted_iota(jnp.int32, (q, 2 * p), 1)
    lo_half = lane < p
    skip = skip_ref[...]
    tm = tm_ref[...]
    tm_t = tmt_ref[...]

    dtr = dtr_ref[...]
    la_c = dtc_ref[...] * (-jnp.exp(alr_ref[...]) * LOG2_E)
    la_r = dtr * (-jnp.exp(alc_ref[...]) * LOG2_E)
    c_hi, c_mid, c_lo = _split3(la_c)
    cum_c = _dot(tm, c_hi) + _dot(tm, c_mid) + _dot(tm, c_lo)
    r_hi, r_mid, r_lo = _split3(la_r)
    cum_r = _dot(r_hi, tm_t) + _dot(r_mid, tm_t) + _dot(r_lo, tm_t)
    tot_r = jnp.sum(la_r, axis=1, keepdims=True)
    ldt_r = jnp.log2(dtr)
    f_r = jnp.exp2(tot_r - cum_r + ldt_r)
    rowb_all = cum_r - ldt_r

    st = st_ref[...]
    for i in range(cps):
        j = i + d * (cps - 1 - 2 * i)
        off = pl.multiple_of(j * q, q)
        x16 = xs_ref[pl.ds(off, q), :]
        b16 = b_ref[pl.ds(off, q), :]
        c16 = c_ref[pl.ds(off, q), :]
        bt = b16.astype(F32).T
        scores = _dot(c16, bt.astype(BF16))
        y_state = _dot(c16, st.astype(BF16))

        y_parts, upd_parts, ec_parts = [], [], []
        for pair in range(heads // 2):
            ms, bs, ecs = [], [], []
            for e in (2 * pair, 2 * pair + 1):
                k = i * heads + e
                colb = jnp.broadcast_to(cum_c[:, k:k + 1], (q, q))
                dec = jnp.exp2((colb - rowb_all[k:k + 1, :]) + madd_ref[...])
                ms.append((scores * dec).astype(BF16))
                bs.append((bt * f_r[k:k + 1, :]).astype(BF16))
                ecs.append(jnp.exp2(colb))
            xp = x16[:, pair * 2 * p:(pair + 1) * 2 * p]
            zero = jnp.zeros_like(xp)
            rhs = jnp.concatenate([jnp.where(lo_half, xp, zero), jnp.where(lo_half, zero, xp)], axis=0)
            y_parts.append(_dot(jnp.concatenate(ms, axis=1), rhs))
            upd_parts.append(_dot(jnp.concatenate(bs, axis=1), rhs))
            ec_parts.append(jnp.where(lo_half, ecs[0], ecs[1]))
        ec = jnp.concatenate(ec_parts, axis=1)
        y = jnp.concatenate(y_parts, axis=1) + y_state * ec + skip * x16.astype(F32)
        y_ref[pl.ds(off, q), :] = y.astype(y_ref.dtype)
        etot = jnp.where(d == 0, ec[q - 1:q, :], ec[0:1, :])
        st = st * etot + jnp.concatenate(upd_parts, axis=1)
    st_ref[...] = st


def _scan_masks():
    q = SSD_CHUNK
    row = jnp.arange(q)[:, None]
    col = jnp.arange(q)[None, :]
    tri = jnp.stack([row >= col, row <= col])
    return (tri.astype(BF16), jnp.swapaxes(tri, 1, 2).astype(BF16),
            jnp.where(tri, 0.0, NEG_BIG).astype(F32))


def _scan_chunks_per_step(l):
    return min(l // SSD_CHUNK, SCAN_CHUNKS_PER_STEP)


def _dt_layouts(dts, heads):
    b, l, _ = dts.shape
    cps = _scan_chunks_per_step(l)
    steps = l // (cps * SSD_CHUNK)
    v = dts.reshape(b, steps, cps, SSD_CHUNK, 2, SSD_GROUPS, heads)
    v = jnp.stack([v[:, :, :, :, 0], v[:, :, ::-1, :, 1]], axis=1)
    col = v.transpose(0, 1, 5, 2, 4, 3, 6).reshape(b, 2 * SSD_GROUPS, steps, SSD_CHUNK, cps * heads)
    row = v.transpose(0, 1, 5, 2, 3, 6, 4).reshape(b, 2 * SSD_GROUPS, steps, cps * heads, SSD_CHUNK)
    return col, row


def _ssd_scan(xbc, dtc5, dtr5, alog, skip512, h0, d_inner):
    b, l, _ = xbc.shape
    g = SSD_GROUPS
    n = SSD_STATE
    gw = d_inner // g
    heads = gw // SSD_HEAD_DIM
    cps = _scan_chunks_per_step(l)
    r = cps * SSD_CHUNK
    steps = l // r
    b_off = d_inner // n
    c_off = (d_inner + g * n) // n
    alog_t = jnp.tile(alog, (1, cps))
    alog_row = alog_t[:, None, :]
    alog_col = alog_t[:, :, None]

    def cb(si, di):
        return si + di * (steps - 1 - 2 * si)

    tm, tm_t, madd = _scan_masks()
    mask_spec = pl.BlockSpec((None, SSD_CHUNK, SSD_CHUNK), lambda bi, di, gi, si: (di, 0, 0))
    return pl.pallas_call(
        functools.partial(_scan_kernel, cps=cps, heads=heads),
        grid=(b, 2, g, steps),
        in_specs=[
            pl.BlockSpec((None, r, gw), lambda bi, di, gi, si: (bi, cb(si, di), gi)),
            pl.BlockSpec((None, r, n), lambda bi, di, gi, si: (bi, cb(si, di), b_off + gi)),
            pl.BlockSpec((None, r, n), lambda bi, di, gi, si: (bi, cb(si, di), c_off + gi)),
            pl.BlockSpec((None, None, None, SSD_CHUNK, cps * heads),
                         lambda bi, di, gi, si: (bi, di * g + gi, cb(si, di), 0, 0)),
            pl.BlockSpec((None, None, None, cps * heads, SSD_CHUNK),
                         lambda bi, di, gi, si: (bi, di * g + gi, cb(si, di), 0, 0)),
            pl.BlockSpec((None, 1, cps * heads), lambda bi, di, gi, si: (di * g + gi, 0, 0)),
            pl.BlockSpec((None, cps * heads, 1), lambda bi, di, gi, si: (di * g + gi, 0, 0)),
            pl.BlockSpec((None, 1, gw), lambda bi, di, gi, si: (di * g + gi, 0, 0)),
            pl.BlockSpec((None, None, None, n, gw), lambda bi, di, gi, si: (bi, di, gi, 0, 0)),
            mask_spec, mask_spec, mask_spec,
        ],
        out_specs=[
            pl.BlockSpec((None, None, r, gw), lambda bi, di, gi, si: (di, bi, cb(si, di), gi)),
            pl.BlockSpec((None, None, None, n, gw), lambda bi, di, gi, si: (bi, di, gi, 0, 0)),
        ],
        out_shape=[jax.ShapeDtypeStruct((2, b, l, d_inner), BF16),
                   jax.ShapeDtypeStruct((b, 2, g, n, gw), F32)],
        compiler_params=_cparams(("parallel", "parallel", "parallel", "arbitrary")),
        name="ssd_scan",
    )(xbc, xbc, xbc, dtc5, dtr5, alog_row, alog_col, skip512, h0, tm, tm_t, madd)


def _ssd_out_kernel(yf_ref, yb_ref, z_ref, nw_ref, w_ref, x_ref, g_ref, o_ref, a_ref, *, gw):
    for gi in range(a_ref.shape[1] // gw):
        sl = slice(gi * gw, (gi + 1) * gw)
        v = (yf_ref[:, sl].astype(F32) + yb_ref[:, sl].astype(F32)) * _silu(z_ref[:, sl].astype(F32))
        ms = jnp.mean(v * v, axis=-1, keepdims=True)
        a_ref[:, sl] = (v * lax.rsqrt(ms + SSD_NORM_EPS) * nw_ref[:, sl]).astype(BF16)
    o_ref[...] = x_ref[...] + g_ref[...] * _dot(a_ref[...], w_ref[...])


def _ssd_out(y2, zx, norm_w, w_out, x, gate):
    b, l, d = x.shape
    k = w_out.shape[0]
    t = _tile(l, 256)
    return pl.pallas_call(
        functools.partial(_ssd_out_kernel, gw=k // SSD_GROUPS),
        grid=(b, l // t),
        in_specs=[
            pl.BlockSpec((None, None, t, k), lambda i, j: (0, i, j, 0)),
            pl.BlockSpec((None, None, t, k), lambda i, j: (1, i, j, 0)),
            pl.BlockSpec((None, t, k), lambda i, j: (i, j, 0)),
            pl.BlockSpec((1, k), lambda i, j: (0, 0)),
            pl.BlockSpec((k, d), lambda i, j: (0, 0), pipeline_mode=pl.Buffered(1)),
            pl.BlockSpec((None, t, d), lambda i, j: (i, j, 0)),
            pl.BlockSpec((None, 1, d), lambda i, j: (i, 0, 0)),
        ],
        out_specs=pl.BlockSpec((None, t, d), lambda i, j: (i, j, 0)),
        out_shape=jax.ShapeDtypeStruct(x.shape, F32),
        scratch_shapes=[pltpu.VMEM((t, k), BF16)],
        compiler_params=_cparams(("parallel", "parallel")),
        name="ssd_out",
    )(y2, y2, zx, norm_w, w_out, x, gate)


def _mm_res_kernel(a_ref, w_ref, x_ref, g_ref, o_ref):
    o_ref[...] = x_ref[...] + g_ref[...] * _dot(a_ref[...], w_ref[...])


def _matmul_residual(a, w, x, gate):
    b, l, d = x.shape
    k = w.shape[0]
    t = _tile(l, 1024)
    tn = _tile(d, 1024)
    return pl.pallas_call(
        _mm_res_kernel,
        grid=(b, l // t, d // tn),
        in_specs=[
            pl.BlockSpec((None, t, k), lambda i, j, n: (i, j, 0)),
            pl.BlockSpec((k, tn), lambda i, j, n: (0, n)),
            pl.BlockSpec((None, t, tn), lambda i, j, n: (i, j, n)),
            pl.BlockSpec((None, 1, tn), lambda i, j, n: (i, 0, n)),
        ],
        out_specs=pl.BlockSpec((None, t, tn), lambda i, j, n: (i, j, n)),
        out_shape=jax.ShapeDtypeStruct(x.shape, F32),
        compiler_params=_cparams(("parallel", "parallel", "parallel")),
        name="matmul_residual",
    )(a, w, x, gate)


def _ffn_kernel(x_ref, nw_ref, sh_ref, sc_ref, g_ref, wg_ref, wu_ref, wd_ref, *rest, final):
    if final:
        fw_ref, o_ref, h_ref = rest
    else:
        o_ref, h_ref = rest
    j = pl.program_id(2)

    @pl.when(j == 0)
    def _():
        h_ref[...] = _ada_norm(x_ref[...], nw_ref[...], sh_ref[...], sc_ref[...]).astype(BF16)
        o_ref[...] = jnp.zeros_like(o_ref)

    h = h_ref[...]
    a = _dot(h, wg_ref[...])
    u = _dot(h, wu_ref[...])
    o_ref[...] += _dot((_silu(a) * u).astype(BF16), wd_ref[...])

    @pl.when(j == pl.num_programs(2) - 1)
    def _():
        out = x_ref[...] + g_ref[...] * o_ref[...]
        if final:
            ms = jnp.mean(out * out, axis=-1, keepdims=True)
            out = out * lax.rsqrt(ms + NORM_EPS) * fw_ref[...]
        o_ref[...] = out


def _ffn(x, nw, shift, scale, gate, wg, wu, wd, final_w=None):
    b, l, d = x.shape
    hdim = wg.shape[1]
    t = _tile(l, 512)
    th = _tile(hdim, 512)
    final = final_w is not None
    vec = pl.BlockSpec((None, 1, d), lambda i, j, k: (i, 0, 0))
    in_specs = [
        pl.BlockSpec((None, t, d), lambda i, j, k: (i, j, 0)),
        pl.BlockSpec((1, d), lambda i, j, k: (0, 0)),
        vec, vec, vec,
        pl.BlockSpec((d, th), lambda i, j, k: (0, k)),
        pl.BlockSpec((d, th), lambda i, j, k: (0, k)),
        pl.BlockSpec((th, d), lambda i, j, k: (k, 0)),
    ]
    args = [x, nw, shift, scale, gate, wg, wu, wd]
    if final:
        in_specs.append(pl.BlockSpec((1, d), lambda i, j, k: (0, 0)))
        args.append(final_w)
    return pl.pallas_call(
        functools.partial(_ffn_kernel, final=final),
        grid=(b, l // t, hdim // th),
        in_specs=in_specs,
        out_specs=pl.BlockSpec((None, t, d), lambda i, j, k: (i, j, 0)),
        out_shape=jax.ShapeDtypeStruct(x.shape, F32),
        scratch_shapes=[pltpu.VMEM((t, d), BF16)],
        compiler_params=_cparams(("parallel", "parallel", "arbitrary")),
        name="ffn",
    )(*args)


def _chan_dft_kernel(x_ref, nw_ref, sh_ref, sc_ref, w_ref, o_ref, *, gw):
    h = _ada_norm(x_ref[...], nw_ref[...], sh_ref[...], sc_ref[...]).astype(BF16)
    for gi in range(h.shape[1] // gw):
        sl = slice(gi * gw, (gi + 1) * gw)
        r = _dot(h[:, sl], w_ref[...])
        o_ref[0, :, sl] = r[:, :gw].astype(BF16)
        o_ref[1, :, sl] = r[:, gw:].astype(BF16)


def _chan_dft(x, nw, shift, scale, w_cs):
    b, l, d = x.shape
    gw = w_cs.shape[0]
    t = _tile(l, 512)
    vec = pl.BlockSpec((None, 1, d), lambda i, j: (i, 0, 0))
    return pl.pallas_call(
        functools.partial(_chan_dft_kernel, gw=gw),
        grid=(b, l // t),
        in_specs=[
            pl.BlockSpec((None, t, d), lambda i, j: (i, j, 0)),
            pl.BlockSpec((1, d), lambda i, j: (0, 0)),
            vec, vec,
            pl.BlockSpec((gw, 2 * gw), lambda i, j: (0, 0)),
        ],
        out_specs=pl.BlockSpec((None, 2, t, d), lambda i, j: (i, 0, j, 0)),
        out_shape=jax.ShapeDtypeStruct((b, 2, l, d), BF16),
        compiler_params=_cparams(("parallel", "parallel")),
        name="chan_dft",
    )(x, nw, shift, scale, w_cs)


def _seq_dft_kernel(a_ref, pq_ref, o_ref, acc_ref):
    k = pl.program_id(3)

    @pl.when(k == 0)
    def _():
        acc_ref[...] = jnp.zeros_like(acc_ref)

    acc_ref[...] += _dot(a_ref[...], pq_ref[...])

    @pl.when(k == pl.num_programs(3) - 1)
    def _():
        o_ref[...] = acc_ref[...].astype(o_ref.dtype)


def _seq_dft(a_mat, pq):
    b, k2, d = pq.shape
    l = a_mat.shape[0]
    tm = _tile(l, 1024)
    tn = _tile(d, 1024)
    tk = _tile(k2, 1024)
    return pl.pallas_call(
        _seq_dft_kernel,
        grid=(b, l // tm, d // tn, k2 // tk),
        in_specs=[
            pl.BlockSpec((tm, tk), lambda i, m, n, k: (m, k)),
            pl.BlockSpec((None, tk, tn), lambda i, m, n, k: (i, k, n)),
        ],
        out_specs=pl.BlockSpec((None, tm, tn), lambda i, m, n, k: (i, m, n)),
        out_shape=jax.ShapeDtypeStruct((b, l, d), BF16),
        scratch_shapes=[pltpu.VMEM((tm, tn), F32)],
        compiler_params=_cparams(("parallel", "parallel", "parallel", "arbitrary")),
        name="seq_dft",
    )(a_mat, pq)


def _dft_tables(n, scale):
    idx = jnp.arange(n, dtype=jnp.int32)

    def cs(prod):
        ang = (prod % n).astype(F32) * (2.0 * math.pi / n)
        return jnp.cos(ang), jnp.sin(ang)

    m = GRID_WIDTH
    if n <= m or n % m:
        c, s = cs(idx[:, None] * idx[None, :])
        return c * scale, s * scale
    ca, sa = cs(idx[:, None] * (m * jnp.arange(n // m, dtype=jnp.int32))[None, :])
    cb, sb = cs(idx[:, None] * jnp.arange(m, dtype=jnp.int32)[None, :])
    ca, sa = ca[:, :, None] * scale, sa[:, :, None] * scale
    cb, sb = cb[:, None, :], sb[:, None, :]
    return (ca * cb - sa * sb).reshape(n, n), (sa * cb + ca * sb).reshape(n, n)


def _pool_kernel(cur_ref, prev_ref, next_ref, nw_ref, sh_ref, sc_ref, g_ref, w_ref, ps_ref,
                 o_ref, ext_ref, *, t, n_tiles, seq_len, gw):
    l = pl.program_id(1)
    nw, sh, sc = nw_ref[...], sh_ref[...], sc_ref[...]
    hc = _ada_norm(cur_ref[...], nw, sh, sc)
    ext_ref[0:HALO_ROWS, :] = jnp.where(l > 0, _ada_norm(prev_ref[...], nw, sh, sc), 0.0)
    ext_ref[HALO_ROWS:HALO_ROWS + t, :] = hc
    ext_ref[HALO_ROWS + t:, :] = jnp.where(l < n_tiles - 1, _ada_norm(next_ref[...], nw, sh, sc), 0.0)
    pos = l * t + lax.broadcasted_iota(jnp.int32, (t, 1), 0)
    for gi, win in enumerate(POOL_WINDOWS):
        sl = slice(gi * gw, (gi + 1) * gw)
        half = win // 2
        tot = ext_ref[pl.ds(HALO_ROWS - half, t), sl]
        for k in range(-half + 1, win - half):
            tot = tot + ext_ref[pl.ds(HALO_ROWS + k, t), sl]
        cnt = (jnp.minimum(pos + (win - half), seq_len) - jnp.maximum(pos - half, 0)).astype(F32)
        pooled = (tot / cnt - hc[:, sl]).astype(BF16)
        out = _dot(pooled, w_ref[gi]) * ps_ref[:, sl]
        o_ref[:, sl] = cur_ref[:, sl] + g_ref[:, sl] * out


def _pool_mixer(x, nw, shift, scale, gate, w_grp, pscale):
    b, l, d = x.shape
    ng, gw, _ = w_grp.shape
    assert max(POOL_WINDOWS) // 2 <= HALO_ROWS
    t = _tile(l, 512)
    n_tiles = l // t
    rb = t // HALO_ROWS
    last_rb = l // HALO_ROWS - 1
    vec = pl.BlockSpec((None, 1, d), lambda i, j: (i, 0, 0))
    return pl.pallas_call(
        functools.partial(_pool_kernel, t=t, n_tiles=n_tiles, seq_len=l, gw=gw),
        grid=(b, n_tiles),
        in_specs=[
            pl.BlockSpec((None, t, d), lambda i, j: (i, j, 0)),
            pl.BlockSpec((None, HALO_ROWS, d), lambda i, j: (i, jnp.maximum(j * rb - 1, 0), 0)),
            pl.BlockSpec((None, HALO_ROWS, d), lambda i, j: (i, jnp.minimum((j + 1) * rb, last_rb), 0)),
            pl.BlockSpec((1, d), lambda i, j: (0, 0)),
            vec, vec, vec,
            pl.BlockSpec((ng, gw, gw), lambda i, j: (0, 0, 0)),
            pl.BlockSpec((1, d), lambda i, j: (0, 0)),
        ],
        out_specs=pl.BlockSpec((None, t, d), lambda i, j: (i, j, 0)),
        out_shape=jax.ShapeDtypeStruct(x.shape, F32),
        scratch_shapes=[pltpu.VMEM((t + 2 * HALO_ROWS, d), F32)],
        compiler_params=_cparams(("parallel", "parallel")),
        name="pool_mixer",
    )(x, x, x, nw, shift, scale, gate, w_grp, pscale)


def _pos_embed(n_tokens, d):
    rows = n_tokens // GRID_WIDTH
    t = jnp.arange(rows * GRID_WIDTH)
    r = (t // GRID_WIDTH).astype(F32)[:, None]
    col = (t % GRID_WIDTH).astype(F32)[:, None]
    quarter = d // 4
    omega = 1.0 / (POS_BASE ** (jnp.arange(quarter, dtype=F32) / quarter))
    return jnp.concatenate([jnp.sin(r * omega), jnp.cos(r * omega),
                            jnp.sin(col * omega), jnp.cos(col * omega)], axis=-1)


def _ssd_layer(x, ctx, mods, mods_c, nw, p, with_ctx):
    sh1, sc1, g1 = mods[0], mods[1], mods[2]
    ch1, cs1, cg1 = mods_c[0], mods_c[1], mods_c[2]
    d_inner = p["out_proj"].shape[0]
    heads = p["dt_bias"].shape[-1]
    hpg = heads // SSD_GROUPS
    gw = d_inner // SSD_GROUPS
    w_main = p["in_proj"][:, :p["in_proj"].shape[1] - 2 * heads]
    w_dt = p["in_proj"][:, p["in_proj"].shape[1] - 2 * heads:]
    dt_bias = p["dt_bias"].reshape(1, 2 * heads)
    alog = p["a_log"].reshape(2 * SSD_GROUPS, hpg)
    skip512 = jnp.repeat(p["d"].reshape(2 * SSD_GROUPS, hpg), SSD_HEAD_DIM, axis=-1)[:, None, :]

    def project(u, shift, scale):
        zx, dts = _in_proj(u, nw, shift, scale, w_main, w_dt, dt_bias)
        xbc = _conv_silu(zx, p["conv_w"], p["conv_b"], d_inner)
        dtc5, dtr5 = _dt_layouts(dts, hpg)
        return zx, xbc, dtc5, dtr5

    zx_c, xbc_c, dtc_c, dtr_c = project(ctx, ch1, cs1)
    zx_l, xbc_l, dtc_l, dtr_l = project(x, sh1, sc1)
    b = x.shape[0]
    h0 = jnp.zeros((b, 2, SSD_GROUPS, SSD_STATE, gw), F32)
    y_c, st_c = _ssd_scan(xbc_c, dtc_c, dtr_c, alog, skip512, h0, d_inner)
    y_l, _ = _ssd_scan(xbc_l, dtc_l, dtr_l, alog, skip512, st_c, d_inner)
    x = _ssd_out(y_l, zx_l, p["norm_w"], p["out_proj"], x, g1)
    if with_ctx:
        ctx = _ssd_out(y_c, zx_c, p["norm_w"], p["out_proj"], ctx, cg1)
    return x, ctx


def _fourier_layer(x, ctx, mods, mods_c, nw, w_out, with_ctx):
    d = x.shape[-1]
    gw = d // N_FOURIER_GROUPS
    cc, sc_ = _dft_tables(gw, gw ** -0.5)
    w_cs = jnp.concatenate([cc, sc_], axis=1).astype(BF16)

    def mix(u, shift, scale, gate):
        l = u.shape[1]
        cl, sl = _dft_tables(l, l ** -0.5)
        a_mat = jnp.concatenate([cl, -sl], axis=1).astype(BF16)
        pq = _chan_dft(u, nw, shift, scale, w_cs)
        f = _seq_dft(a_mat, pq.reshape(u.shape[0], 2 * l, d))
        return _matmul_residual(f, w_out, u, gate)

    x = mix(x, mods[0], mods[1], mods[2])
    if with_ctx:
        ctx = mix(ctx, mods_c[0], mods_c[1], mods_c[2])
    return x, ctx


def kernel(x, c, ctx, c_ctx, w_mod, b_mod, norm_w, ffn_w_gate, ffn_w_up, ffn_w_down, ssd_in_proj, ssd_conv_w,
           ssd_conv_b, ssd_dt_bias, ssd_a_log, ssd_d, ssd_norm_w, ssd_out_proj, fourier_w_out, pool_w, pool_scale,
           final_norm_w):
    b, l, d = x.shape
    depth = w_mod.shape[0]
    n_ctx_rows = V7X_SUBLANES
    c_all = jnp.concatenate([c, jnp.broadcast_to(c_ctx[None, :], (n_ctx_rows, d))], axis=0)
    mod_all = _modulation(c_all, w_mod, b_mod)

    x = _add_pos(x, _pos_embed(l, d))
    for i in range(depth):
        kind, j = i % N_MIXERS, i // N_MIXERS
        last = i == depth - 1
        need_ctx = (not last) or kind == 0
        mods = [mod_all[i, :b, None, k * d:(k + 1) * d] for k in range(6)]
        mods_c = [jnp.broadcast_to(mod_all[i, b:b + 1, None, k * d:(k + 1) * d], (b, 1, d)) for k in range(6)]
        nw1 = norm_w[i, 0][None, :]
        nw2 = norm_w[i, 1][None, :]
        if kind == 0:
            p = dict(in_proj=ssd_in_proj[j].astype(BF16), conv_w=ssd_conv_w[j], conv_b=ssd_conv_b[j][None, :],
                     dt_bias=ssd_dt_bias[j], a_log=ssd_a_log[j], d=ssd_d[j], norm_w=ssd_norm_w[j][None, :],
                     out_proj=ssd_out_proj[j].astype(BF16))
            x, ctx = _ssd_layer(x, ctx, mods, mods_c, nw1, p, not last)
        elif kind == 1:
            x, ctx = _fourier_layer(x, ctx, mods, mods_c, nw1, fourier_w_out[j].astype(BF16), not last)
        else:
            pw = pool_w[j].astype(BF16)
            ps = pool_scale[j][None, :]
            x = _pool_mixer(x, nw1, mods[0], mods[1], mods[2], pw, ps)
            if not last:
                ctx = _pool_mixer(ctx, nw1, mods_c[0], mods_c[1], mods_c[2], pw, ps)
        wg, wu, wd = ffn_w_gate[i].astype(BF16), ffn_w_up[i].astype(BF16), ffn_w_down[i].astype(BF16)
        fw = final_norm_w[None, :] if last else None
        x = _ffn(x, nw2, mods[3], mods[4], mods[5], wg, wu, wd, fw)
        if not last:
            ctx = _ffn(ctx, nw2, mods_c[3], mods_c[4], mods_c[5], wg, wu, wd)
        del need_ctx
    return x
```

```python
import functools
import math

import jax
import jax.numpy as jnp
from jax import lax
from jax.experimental import pallas as pl
from jax.experimental.pallas import tpu as pltpu

F32 = jnp.float32
BF16 = jnp.bfloat16

GRID_WIDTH = 64
POS_BASE = 10000.0
NORM_EPS = 1e-6
SSD_NORM_EPS = 1e-5
SSD_HEAD_DIM = 64
SSD_GROUPS = 8
SSD_STATE = 128
SSD_CHUNK = 128
N_FOURIER_GROUPS = 4
POOL_WINDOWS = (2, 4, 8, 16)
N_MIXERS = 3
SCAN_CHUNKS_PER_STEP = 8

V7X_LANES = 128
V7X_SUBLANES = 8
V7X_VMEM_BYTES = 64 * 1024 * 1024
V7X_VMEM_LIMIT_BYTES = V7X_VMEM_BYTES - 8 * 1024 * 1024
HALO_ROWS = V7X_SUBLANES
NEG_BIG = -1e30
LOG2_E = 1.4426950408889634


def _cparams(semantics):
    return pltpu.CompilerParams(dimension_semantics=semantics,
                                vmem_limit_bytes=V7X_VMEM_LIMIT_BYTES)


def _tile(n, pref):
    t = min(n, pref)
    assert n % t == 0, (n, t)
    return t


def _silu(v):
    h = 0.5 * v
    return h + h * jnp.tanh(h)


def _ada_norm(x, nw, shift, scale):
    ms = jnp.mean(x * x, axis=-1, keepdims=True)
    y = x * lax.rsqrt(ms + NORM_EPS)
    return (y * nw) * (1.0 + scale) + shift


def _pos_tile(rt_ref, ct_ref, tile_idx, t):
    w = GRID_WIDTH
    n_rows = t // w
    half = rt_ref.shape[1]
    r0 = tile_idx * n_rows
    left = jnp.concatenate([jnp.broadcast_to(rt_ref[pl.ds(r0 + rr, 1), :], (w, half)) for rr in range(n_rows)],
                           axis=0)
    right = jnp.concatenate([ct_ref[...]] * n_rows, axis=0)
    return jnp.concatenate([left, right], axis=1)


def _split3(v):
    hi = v.astype(BF16)
    r1 = v - hi.astype(F32)
    mid = r1.astype(BF16)
    lo = (r1 - mid.astype(F32)).astype(BF16)
    return hi, mid, lo


def _dot(a, b):
    return jnp.dot(a, b, preferred_element_type=F32)


def _mod_kernel(c_ref, w_ref, b_ref, o_ref):
    s = _silu(c_ref[...])
    o_ref[...] = jnp.dot(s, w_ref[...], preferred_element_type=F32,
                         precision=lax.Precision.HIGHEST) + b_ref[...]


def _modulation(c_all, w_mod, b_mod):
    depth, d, n = w_mod.shape
    rows = c_all.shape[0]
    tn = _tile(n, 1024)
    return pl.pallas_call(
        _mod_kernel,
        grid=(depth, n // tn),
        in_specs=[
            pl.BlockSpec((rows, d), lambda i, j: (0, 0)),
            pl.BlockSpec((None, d, tn), lambda i, j: (i, 0, j)),
            pl.BlockSpec((None, 1, tn), lambda i, j: (i, 0, j)),
        ],
        out_specs=pl.BlockSpec((None, rows, tn), lambda i, j: (i, 0, j)),
        out_shape=jax.ShapeDtypeStruct((depth, rows, n), F32),
        compiler_params=_cparams(("parallel", "parallel")),
        name="modulation",
    )(c_all, w_mod, b_mod.reshape(depth, 1, n))


def _in_proj_kernel(x_ref, *rest, with_pos):
    if with_pos:
        rt_ref, ct_ref, nw_ref, sh_ref, sc_ref, w_ref, wdt_ref, b_ref, o_ref, dt_ref, h_ref = rest
    else:
        nw_ref, sh_ref, sc_ref, w_ref, wdt_ref, b_ref, o_ref, dt_ref, h_ref = rest

    @pl.when(pl.program_id(2) == 0)
    def _():
        xv = x_ref[...]
        if with_pos:
            xv = xv + _pos_tile(rt_ref, ct_ref, pl.program_id(1), xv.shape[0])
        h = _ada_norm(xv, nw_ref[...], sh_ref[...], sc_ref[...]).astype(BF16)
        h_ref[...] = h
        v = _dot(h, wdt_ref[...]) + b_ref[...]
        dt_ref[...] = jnp.maximum(v, 0.0) + jnp.log1p(jnp.exp(-jnp.abs(v)))

    o_ref[...] = _dot(h_ref[...], w_ref[...]).astype(o_ref.dtype)


def _in_proj(x, pos, nw, shift, scale, w, w_dt, dt_bias):
    b, l, d = x.shape
    n = w.shape[1]
    ndt = w_dt.shape[1]
    t = _tile(l, 1024)
    tn = _tile(n, 1024)
    with_pos = pos is not None
    pos_args = list(pos) if with_pos else []
    pos_specs = [pl.BlockSpec(a.shape, lambda i, j, k: (0, 0)) for a in pos_args]
    assert not with_pos or t % GRID_WIDTH == 0
    return pl.pallas_call(
        functools.partial(_in_proj_kernel, with_pos=with_pos),
        grid=(b, l // t, n // tn),
        in_specs=[pl.BlockSpec((None, t, d), lambda i, j, k: (i, j, 0))] + pos_specs + [
            pl.BlockSpec((1, d), lambda i, j, k: (0, 0)),
            pl.BlockSpec((None, 1, d), lambda i, j, k: (i, 0, 0)),
            pl.BlockSpec((None, 1, d), lambda i, j, k: (i, 0, 0)),
            pl.BlockSpec((d, tn), lambda i, j, k: (0, k)),
            pl.BlockSpec((d, ndt), lambda i, j, k: (0, 0)),
            pl.BlockSpec((1, ndt), lambda i, j, k: (0, 0)),
        ],
        out_specs=[pl.BlockSpec((None, t, tn), lambda i, j, k: (i, j, k)),
                   pl.BlockSpec((None, t, ndt), lambda i, j, k: (i, j, 0))],
        out_shape=[jax.ShapeDtypeStruct((b, l, n), BF16), jax.ShapeDtypeStruct((b, l, ndt), F32)],
        scratch_shapes=[pltpu.VMEM((t, d), BF16)],
        compiler_params=_cparams(("parallel", "parallel", "arbitrary")),
        name="in_proj",
    )(x, *pos_args, nw, shift, scale, w, w_dt, dt_bias)


def _conv_kernel(cur_ref, prev_ref, next_ref, sel_ref, w_ref, b_ref, o_ref, *, t, n_tiles, width):
    l = pl.program_id(1)
    pad = width // 2
    halo = prev_ref.shape[0]
    sb = sel_ref.shape[1]
    zero = jnp.zeros(prev_ref.shape, prev_ref.dtype)
    prev = jnp.where(l > 0, prev_ref[...], zero)
    nxt = jnp.where(l < n_tiles - 1, next_ref[...], zero)
    ext = jnp.concatenate([prev, cur_ref[...], nxt], axis=0)
    for s in range(t // sb):
        win = ext[s * sb:(s + 1) * sb + 2 * halo]
        acc = b_ref[...] + w_ref[pad:pad + 1, :] * win[halo:halo + sb].astype(F32)
        for k in range(width):
            if k != pad:
                acc = acc + w_ref[k:k + 1, :] * _dot(sel_ref[k], win)
        o_ref[s * sb:(s + 1) * sb, :] = _silu(acc).astype(o_ref.dtype)


def _conv_silu(zx, conv_w, conv_b, col_offset):
    b, l, _ = zx.shape
    width, c = conv_w.shape
    pad = width // 2
    halo = 2 * V7X_SUBLANES
    assert pad <= halo
    t = _tile(l, 512)
    tc = _tile(c, 1024)
    sb = _tile(t, V7X_LANES)
    n_tiles = l // t
    off = col_offset // tc
    hb = t // halo
    last_hb = l // halo - 1
    i_idx = jnp.arange(sb)[None, :, None]
    j_idx = jnp.arange(sb + 2 * halo)[None, None, :]
    k_idx = jnp.arange(width)[:, None, None]
    sel = (j_idx == i_idx + halo + k_idx - pad).astype(BF16)
    return pl.pallas_call(
        functools.partial(_conv_kernel, t=t, n_tiles=n_tiles, width=width),
        grid=(b, n_tiles, c // tc),
        in_specs=[
            pl.BlockSpec((None, t, tc), lambda i, j, k: (i, j, off + k)),
            pl.BlockSpec((None, halo, tc), lambda i, j, k: (i, jnp.maximum(j * hb - 1, 0), off + k)),
            pl.BlockSpec((None, halo, tc), lambda i, j, k: (i, jnp.minimum((j + 1) * hb, last_hb), off + k)),
            pl.BlockSpec((width, sb, sb + 2 * halo), lambda i, j, k: (0, 0, 0)),
            pl.BlockSpec((width, tc), lambda i, j, k: (0, k)),
            pl.BlockSpec((1, tc), lambda i, j, k: (0, k)),
        ],
        out_specs=pl.BlockSpec((None, t, tc), lambda i, j, k: (i, j, k)),
        out_shape=jax.ShapeDtypeStruct((b, l, c), BF16),
        compiler_params=_cparams(("parallel", "parallel", "parallel")),
        name="conv_silu",
    )(zx, zx, zx, sel, conv_w, conv_b)


def _scan_kernel(xs_ref, b_ref, c_ref, dtc_ref, dtr_ref, alr_ref, alc_ref, skd_ref, h0_ref,
                 tm_ref, tmt_ref, madd_ref, y_ref, st_ref, *, cps, heads):
    d = pl.program_id(1)
    s = pl.program_id(3)
    q = SSD_CHUNK
    p = SSD_HEAD_DIM

    @pl.when(s == 0)
    def _():
        st_ref[...] = h0_ref[...]

    lane = lax.broadcasted_iota(jnp.int32, (q, 2 * p), 1)
    lo_half = lane < p
    tm = tm_ref[...]
    tm_t = tmt_ref[...]

    dtr = dtr_ref[...]
    la_c = dtc_ref[...] * (-jnp.exp(alr_ref[...]) * LOG2_E)
    la_r = dtr * (-jnp.exp(alc_ref[...]) * LOG2_E)
    c_hi, c_mid, c_lo = _split3(la_c)
    cum_c = _dot(tm, c_hi) + _dot(tm, c_mid) + _dot(tm, c_lo)
    r_hi, r_mid, r_lo = _split3(la_r)
    cum_r = _dot(r_hi, tm_t) + _dot(r_mid, tm_t) + _dot(r_lo, tm_t)
    tot_r = jnp.sum(la_r, axis=1, keepdims=True)
    ldt_r = jnp.log2(dtr)
    f_r = jnp.exp2(tot_r - cum_r + ldt_r)
    rowb_all = cum_r - ldt_r

    st = st_ref[...]
    for i in range(cps):
        j = i + d * (cps - 1 - 2 * i)
        off = pl.multiple_of(j * q, q)
        x16 = xs_ref[pl.ds(off, q), :]
        b16 = b_ref[pl.ds(off, q), :]
        c16 = c_ref[pl.ds(off, q), :]
        bt16 = b16.astype(F32).T.astype(BF16)
        scores16 = _dot(c16, bt16).astype(BF16)
        y_state = _dot(c16, st.astype(BF16))

        y_parts, upd_parts, ec_parts = [], [], []
        for pair in range(heads // 2):
            ms, bs, ecs = [], [], []
            for e in (2 * pair, 2 * pair + 1):
                k = i * heads + e
                colb = jnp.broadcast_to(cum_c[:, k:k + 1], (q, q))
                dec = jnp.exp2((colb - rowb_all[k:k + 1, :]) + madd_ref[...])
                ms.append(scores16 * dec.astype(BF16) + skd_ref[e])
                bs.append(bt16 * jnp.broadcast_to(f_r[k:k + 1, :], (q, q)).astype(BF16))
                ecs.append(jnp.exp2(colb))
            xp = x16[:, pair * 2 * p:(pair + 1) * 2 * p]
            zero = jnp.zeros_like(xp)
            rhs = jnp.concatenate([jnp.where(lo_half, xp, zero), jnp.where(lo_half, zero, xp)], axis=0)
            y_parts.append(_dot(jnp.concatenate(ms, axis=1), rhs))
            upd_parts.append(_dot(jnp.concatenate(bs, axis=1), rhs))
            ec_parts.append(jnp.where(lo_half, ecs[0], ecs[1]))
        ec = jnp.concatenate(ec_parts, axis=1)
        y = jnp.concatenate(y_parts, axis=1) + y_state * ec
        y_ref[pl.ds(off, q), :] = y.astype(y_ref.dtype)
        etot = jnp.where(d == 0, ec[q - 1:q, :], ec[0:1, :])
        st = st * etot + jnp.concatenate(upd_parts, axis=1)
    st_ref[...] = st


def _scan_masks():
    q = SSD_CHUNK
    row = jnp.arange(q)[:, None]
    col = jnp.arange(q)[None, :]
    tri = jnp.stack([row >= col, row <= col])
    return (tri.astype(BF16), jnp.swapaxes(tri, 1, 2).astype(BF16),
            jnp.where(tri, 0.0, NEG_BIG).astype(F32))


def _scan_chunks_per_step(l):
    return min(l // SSD_CHUNK, SCAN_CHUNKS_PER_STEP)


def _dt_layouts(dts, heads):
    b, l, _ = dts.shape
    cps = _scan_chunks_per_step(l)
    steps = l // (cps * SSD_CHUNK)
    v = dts.reshape(b, steps, cps, SSD_CHUNK, 2, SSD_GROUPS, heads)
    v = jnp.stack([v[:, :, :, :, 0], v[:, :, ::-1, :, 1]], axis=1)
    col = v.transpose(0, 1, 5, 2, 4, 3, 6).reshape(b, 2 * SSD_GROUPS, steps, SSD_CHUNK, cps * heads)
    row = v.transpose(0, 1, 5, 2, 3, 6, 4).reshape(b, 2 * SSD_GROUPS, steps, cps * heads, SSD_CHUNK)
    return col, row


def _ssd_scan(xbc, dtc5, dtr5, alog, skip_diag, h0, d_inner):
    b, l, _ = xbc.shape
    g = SSD_GROUPS
    n = SSD_STATE
    gw = d_inner // g
    heads = gw // SSD_HEAD_DIM
    cps = _scan_chunks_per_step(l)
    r = cps * SSD_CHUNK
    steps = l // r
    b_off = d_inner // n
    c_off = (d_inner + g * n) // n
    alog_t = jnp.tile(alog, (1, cps))
    alog_row = alog_t[:, None, :]
    alog_col = alog_t[:, :, None]

    def cb(si, di):
        return si + di * (steps - 1 - 2 * si)

    tm, tm_t, madd = _scan_masks()
    mask_spec = pl.BlockSpec((None, SSD_CHUNK, SSD_CHUNK), lambda bi, di, gi, si: (di, 0, 0))
    return pl.pallas_call(
        functools.partial(_scan_kernel, cps=cps, heads=heads),
        grid=(b, 2, g, steps),
        in_specs=[
            pl.BlockSpec((None, r, gw), lambda bi, di, gi, si: (bi, cb(si, di), gi)),
            pl.BlockSpec((None, r, n), lambda bi, di, gi, si: (bi, cb(si, di), b_off + gi)),
            pl.BlockSpec((None, r, n), lambda bi, di, gi, si: (bi, cb(si, di), c_off + gi)),
            pl.BlockSpec((None, None, None, SSD_CHUNK, cps * heads),
                         lambda bi, di, gi, si: (bi, di * g + gi, cb(si, di), 0, 0)),
            pl.BlockSpec((None, None, None, cps * heads, SSD_CHUNK),
                         lambda bi, di, gi, si: (bi, di * g + gi, cb(si, di), 0, 0)),
            pl.BlockSpec((None, 1, cps * heads), lambda bi, di, gi, si: (di * g + gi, 0, 0)),
            pl.BlockSpec((None, cps * heads, 1), lambda bi, di, gi, si: (di * g + gi, 0, 0)),
            pl.BlockSpec((None, heads, SSD_CHUNK, SSD_CHUNK), lambda bi, di, gi, si: (di * g + gi, 0, 0, 0)),
            pl.BlockSpec((None, None, None, n, gw), lambda bi, di, gi, si: (bi, di, gi, 0, 0)),
            mask_spec, mask_spec, mask_spec,
        ],
        out_specs=[
            pl.BlockSpec((None, None, r, gw), lambda bi, di, gi, si: (di, bi, cb(si, di), gi)),
            pl.BlockSpec((None, None, None, n, gw), lambda bi, di, gi, si: (bi, di, gi, 0, 0)),
        ],
        out_shape=[jax.ShapeDtypeStruct((2, b, l, d_inner), BF16),
                   jax.ShapeDtypeStruct((b, 2, g, n, gw), F32)],
        compiler_params=_cparams(("parallel", "parallel", "parallel", "arbitrary")),
        name="ssd_scan",
    )(xbc, xbc, xbc, dtc5, dtr5, alog_row, alog_col, skip_diag, h0, tm, tm_t, madd)


def _ssd_out_kernel(yf_ref, yb_ref, z_ref, nw_ref, w_ref, x_ref, g_ref, *rest, gw, with_pos):
    if with_pos:
        rt_ref, ct_ref, o_ref, a_ref = rest
    else:
        o_ref, a_ref = rest
    for gi in range(a_ref.shape[1] // gw):
        sl = slice(gi * gw, (gi + 1) * gw)
        v = (yf_ref[:, sl].astype(F32) + yb_ref[:, sl].astype(F32)) * _silu(z_ref[:, sl].astype(F32))
        ms = jnp.mean(v * v, axis=-1, keepdims=True)
        a_ref[:, sl] = (v * lax.rsqrt(ms + SSD_NORM_EPS) * nw_ref[:, sl]).astype(BF16)
    xv = x_ref[...]
    if with_pos:
        xv = xv + _pos_tile(rt_ref, ct_ref, pl.program_id(1), xv.shape[0])
    o_ref[...] = xv + g_ref[...] * _dot(a_ref[...], w_ref[...])


def _ssd_out(y2, zx, norm_w, w_out, x, gate, pos=None):
    b, l, d = x.shape
    k = w_out.shape[0]
    t = _tile(l, 256)
    with_pos = pos is not None
    pos_args = list(pos) if with_pos else []
    pos_specs = [pl.BlockSpec(a.shape, lambda i, j: (0, 0)) for a in pos_args]
    assert not with_pos or t % GRID_WIDTH == 0
    return pl.pallas_call(
        functools.partial(_ssd_out_kernel, gw=k // SSD_GROUPS, with_pos=with_pos),
        grid=(b, l // t),
        in_specs=[
            pl.BlockSpec((None, None, t, k), lambda i, j: (0, i, j, 0)),
            pl.BlockSpec((None, None, t, k), lambda i, j: (1, i, j, 0)),
            pl.BlockSpec((None, t, k), lambda i, j: (i, j, 0)),
            pl.BlockSpec((1, k), lambda i, j: (0, 0)),
            pl.BlockSpec((k, d), lambda i, j: (0, 0), pipeline_mode=pl.Buffered(1)),
            pl.BlockSpec((None, t, d), lambda i, j: (i, j, 0)),
            pl.BlockSpec((None, 1, d), lambda i, j: (i, 0, 0)),
        ] + pos_specs,
        out_specs=pl.BlockSpec((None, t, d), lambda i, j: (i, j, 0)),
        out_shape=jax.ShapeDtypeStruct(x.shape, F32),
        scratch_shapes=[pltpu.VMEM((t, k), BF16)],
        compiler_params=_cparams(("parallel", "parallel")),
        name="ssd_out",
    )(y2, y2, zx, norm_w, w_out, x, gate, *pos_args)


def _mm_res_kernel(a_ref, w_ref, x_ref, g_ref, o_ref):
    o_ref[...] = x_ref[...] + g_ref[...] * _dot(a_ref[...], w_ref[...])


def _matmul_residual(a, w, x, gate):
    b, l, d = x.shape
    k = w.shape[0]
    t = _tile(l, 1024)
    tn = _tile(d, 1024)
    return pl.pallas_call(
        _mm_res_kernel,
        grid=(b, l // t, d // tn),
        in_specs=[
            pl.BlockSpec((None, t, k), lambda i, j, n: (i, j, 0)),
            pl.BlockSpec((k, tn), lambda i, j, n: (0, n)),
            pl.BlockSpec((None, t, tn), lambda i, j, n: (i, j, n)),
            pl.BlockSpec((None, 1, tn), lambda i, j, n: (i, 0, n)),
        ],
        out_specs=pl.BlockSpec((None, t, tn), lambda i, j, n: (i, j, n)),
        out_shape=jax.ShapeDtypeStruct(x.shape, F32),
        compiler_params=_cparams(("parallel", "parallel", "parallel")),
        name="matmul_residual",
    )(a, w, x, gate)


def _ffn_kernel(x_ref, nw_ref, sh_ref, sc_ref, g_ref, wg_ref, wu_ref, wd_ref, *rest, final):
    if final:
        fw_ref, o_ref, h_ref = rest
    else:
        o_ref, h_ref = rest
    j = pl.program_id(2)

    @pl.when(j == 0)
    def _():
        h_ref[...] = _ada_norm(x_ref[...], nw_ref[...], sh_ref[...], sc_ref[...]).astype(BF16)
        o_ref[...] = jnp.zeros_like(o_ref)

    h = h_ref[...]
    a = _dot(h, wg_ref[...])
    u = _dot(h, wu_ref[...])
    o_ref[...] += _dot((_silu(a) * u).astype(BF16), wd_ref[...])

    @pl.when(j == pl.num_programs(2) - 1)
    def _():
        out = x_ref[...] + g_ref[...] * o_ref[...]
        if final:
            ms = jnp.mean(out * out, axis=-1, keepdims=True)
            out = out * lax.rsqrt(ms + NORM_EPS) * fw_ref[...]
        o_ref[...] = out


def _ffn(x, nw, shift, scale, gate, wg, wu, wd, final_w=None):
    b, l, d = x.shape
    hdim = wg.shape[1]
    t = _tile(l, 512)
    th = _tile(hdim, 512)
    final = final_w is not None
    vec = pl.BlockSpec((None, 1, d), lambda i, j, k: (i, 0, 0))
    in_specs = [
        pl.BlockSpec((None, t, d), lambda i, j, k: (i, j, 0)),
        pl.BlockSpec((1, d), lambda i, j, k: (0, 0)),
        vec, vec, vec,
        pl.BlockSpec((d, th), lambda i, j, k: (0, k)),
        pl.BlockSpec((d, th), lambda i, j, k: (0, k)),
        pl.BlockSpec((th, d), lambda i, j, k: (k, 0)),
    ]
    args = [x, nw, shift, scale, gate, wg, wu, wd]
    if final:
        in_specs.append(pl.BlockSpec((1, d), lambda i, j, k: (0, 0)))
        args.append(final_w)
    return pl.pallas_call(
        functools.partial(_ffn_kernel, final=final),
        grid=(b, l // t, hdim // th),
        in_specs=in_specs,
        out_specs=pl.BlockSpec((None, t, d), lambda i, j, k: (i, j, 0)),
        out_shape=jax.ShapeDtypeStruct(x.shape, F32),
        scratch_shapes=[pltpu.VMEM((t, d), BF16)],
        compiler_params=_cparams(("parallel", "parallel", "arbitrary")),
        name="ffn",
    )(*args)


def _chan_dft_kernel(x_ref, nw_ref, sh_ref, sc_ref, w_ref, o_ref, *, gw):
    h = _ada_norm(x_ref[...], nw_ref[...], sh_ref[...], sc_ref[...]).astype(BF16)
    for gi in range(h.shape[1] // gw):
        sl = slice(gi * gw, (gi + 1) * gw)
        r = _dot(h[:, sl], w_ref[...])
        o_ref[0, :, sl] = r[:, :gw].astype(BF16)
        o_ref[1, :, sl] = r[:, gw:].astype(BF16)


def _chan_dft(x, nw, shift, scale, w_cs):
    b, l, d = x.shape
    gw = w_cs.shape[0]
    t = _tile(l, 512)
    vec = pl.BlockSpec((None, 1, d), lambda i, j: (i, 0, 0))
    return pl.pallas_call(
        functools.partial(_chan_dft_kernel, gw=gw),
        grid=(b, l // t),
        in_specs=[
            pl.BlockSpec((None, t, d), lambda i, j: (i, j, 0)),
            pl.BlockSpec((1, d), lambda i, j: (0, 0)),
            vec, vec,
            pl.BlockSpec((gw, 2 * gw), lambda i, j: (0, 0)),
        ],
        out_specs=pl.BlockSpec((None, 2, t, d), lambda i, j: (i, 0, j, 0)),
        out_shape=jax.ShapeDtypeStruct((b, 2, l, d), BF16),
        compiler_params=_cparams(("parallel", "parallel")),
        name="chan_dft",
    )(x, nw, shift, scale, w_cs)


def _seq_dft_kernel(a_ref, pq_ref, o_ref, acc_ref):
    k = pl.program_id(3)

    @pl.when(k == 0)
    def _():
        acc_ref[...] = jnp.zeros_like(acc_ref)

    acc_ref[...] += _dot(a_ref[...], pq_ref[...])

    @pl.when(k == pl.num_programs(3) - 1)
    def _():
        o_ref[...] = acc_ref[...].astype(o_ref.dtype)


def _seq_dft(a_mat, pq):
    b, k2, d = pq.shape
    l = a_mat.shape[0]
    tm = _tile(l, 1024)
    tn = _tile(d, 1024)
    tk = _tile(k2, 1024)
    return pl.pallas_call(
        _seq_dft_kernel,
        grid=(b, l // tm, d // tn, k2 // tk),
        in_specs=[
            pl.BlockSpec((tm, tk), lambda i, m, n, k: (m, k)),
            pl.BlockSpec((None, tk, tn), lambda i, m, n, k: (i, k, n)),
        ],
        out_specs=pl.BlockSpec((None, tm, tn), lambda i, m, n, k: (i, m, n)),
        out_shape=jax.ShapeDtypeStruct((b, l, d), BF16),
        scratch_shapes=[pltpu.VMEM((tm, tn), F32)],
        compiler_params=_cparams(("parallel", "parallel", "parallel", "arbitrary")),
        name="seq_dft",
    )(a_mat, pq)


def _dft_tables(n, scale):
    idx = jnp.arange(n, dtype=jnp.int32)

    def cs(prod):
        ang = (prod % n).astype(F32) * (2.0 * math.pi / n)
        return jnp.cos(ang), jnp.sin(ang)

    m = GRID_WIDTH
    if n <= m or n % m:
        c, s = cs(idx[:, None] * idx[None, :])
        return c * scale, s * scale
    ca, sa = cs(idx[:, None] * (m * jnp.arange(n // m, dtype=jnp.int32))[None, :])
    cb, sb = cs(idx[:, None] * jnp.arange(m, dtype=jnp.int32)[None, :])
    ca, sa = ca[:, :, None] * scale, sa[:, :, None] * scale
    cb, sb = cb[:, None, :], sb[:, None, :]
    return (ca * cb - sa * sb).reshape(n, n), (sa * cb + ca * sb).reshape(n, n)


def _pool_kernel(cur_ref, prev_ref, next_ref, nw_ref, sh_ref, sc_ref, g_ref, w_ref, ps_ref,
                 o_ref, ext_ref, *, t, n_tiles, seq_len, gw):
    l = pl.program_id(1)
    nw, sh, sc = nw_ref[...], sh_ref[...], sc_ref[...]
    hc = _ada_norm(cur_ref[...], nw, sh, sc)
    ext_ref[0:HALO_ROWS, :] = jnp.where(l > 0, _ada_norm(prev_ref[...], nw, sh, sc), 0.0)
    ext_ref[HALO_ROWS:HALO_ROWS + t, :] = hc
    ext_ref[HALO_ROWS + t:, :] = jnp.where(l < n_tiles - 1, _ada_norm(next_ref[...], nw, sh, sc), 0.0)
    pos = l * t + lax.broadcasted_iota(jnp.int32, (t, 1), 0)
    for gi, win in enumerate(POOL_WINDOWS):
        sl = slice(gi * gw, (gi + 1) * gw)
        half = win // 2
        tot = ext_ref[pl.ds(HALO_ROWS - half, t), sl]
        for k in range(-half + 1, win - half):
            tot = tot + ext_ref[pl.ds(HALO_ROWS + k, t), sl]
        cnt = (jnp.minimum(pos + (win - half), seq_len) - jnp.maximum(pos - half, 0)).astype(F32)
        pooled = (tot / cnt - hc[:, sl]).astype(BF16)
        out = _dot(pooled, w_ref[gi]) * ps_ref[:, sl]
        o_ref[:, sl] = cur_ref[:, sl] + g_ref[:, sl] * out


def _pool_mixer(x, nw, shift, scale, gate, w_grp, pscale):
    b, l, d = x.shape
    ng, gw, _ = w_grp.shape
    assert max(POOL_WINDOWS) // 2 <= HALO_ROWS
    t = _tile(l, 512)
    n_tiles = l // t
    rb = t // HALO_ROWS
    last_rb = l // HALO_ROWS - 1
    vec = pl.BlockSpec((None, 1, d), lambda i, j: (i, 0, 0))
    return pl.pallas_call(
        functools.partial(_pool_kernel, t=t, n_tiles=n_tiles, seq_len=l, gw=gw),
        grid=(b, n_tiles),
        in_specs=[
            pl.BlockSpec((None, t, d), lambda i, j: (i, j, 0)),
            pl.BlockSpec((None, HALO_ROWS, d), lambda i, j: (i, jnp.maximum(j * rb - 1, 0), 0)),
            pl.BlockSpec((None, HALO_ROWS, d), lambda i, j: (i, jnp.minimum((j + 1) * rb, last_rb), 0)),
            pl.BlockSpec((1, d), lambda i, j: (0, 0)),
            vec, vec, vec,
            pl.BlockSpec((ng, gw, gw), lambda i, j: (0, 0, 0)),
            pl.BlockSpec((1, d), lambda i, j: (0, 0)),
        ],
        out_specs=pl.BlockSpec((None, t, d), lambda i, j: (i, j, 0)),
        out_shape=jax.ShapeDtypeStruct(x.shape, F32),
        scratch_shapes=[pltpu.VMEM((t + 2 * HALO_ROWS, d), F32)],
        compiler_params=_cparams(("parallel", "parallel")),
        name="pool_mixer",
    )(x, x, x, nw, shift, scale, gate, w_grp, pscale)


def _pos_tables(n_tokens, d):
    rows = n_tokens // GRID_WIDTH
    quarter = d // 4
    omega = 1.0 / (POS_BASE ** (jnp.arange(quarter, dtype=F32) / quarter))

    def table(n):
        v = jnp.arange(n, dtype=F32)[:, None] * omega
        return jnp.concatenate([jnp.sin(v), jnp.cos(v)], axis=-1)

    return table(rows), table(GRID_WIDTH)


def _ssd_layer(x, ctx, mods, mods_c, nw, p, with_ctx, pos=None):
    sh1, sc1, g1 = mods[0], mods[1], mods[2]
    ch1, cs1, cg1 = mods_c[0], mods_c[1], mods_c[2]
    d_inner = p["out_proj"].shape[0]
    heads = p["dt_bias"].shape[-1]
    hpg = heads // SSD_GROUPS
    gw = d_inner // SSD_GROUPS
    w_main = p["in_proj"][:, :p["in_proj"].shape[1] - 2 * heads]
    w_dt = p["in_proj"][:, p["in_proj"].shape[1] - 2 * heads:]
    dt_bias = p["dt_bias"].reshape(1, 2 * heads)
    alog = p["a_log"].reshape(2 * SSD_GROUPS, hpg)
    skip_diag = (p["d"].reshape(2 * SSD_GROUPS, hpg)[:, :, None, None]
               * jnp.eye(SSD_CHUNK, dtype=F32)[None, None]).astype(BF16)

    def project(u, shift, scale, u_pos=None):
        zx, dts = _in_proj(u, u_pos, nw, shift, scale, w_main, w_dt, dt_bias)
        xbc = _conv_silu(zx, p["conv_w"], p["conv_b"], d_inner)
        dtc5, dtr5 = _dt_layouts(dts, hpg)
        return zx, xbc, dtc5, dtr5

    zx_c, xbc_c, dtc_c, dtr_c = project(ctx, ch1, cs1)
    zx_l, xbc_l, dtc_l, dtr_l = project(x, sh1, sc1, pos)
    b = x.shape[0]
    h0 = jnp.zeros((b, 2, SSD_GROUPS, SSD_STATE, gw), F32)
    y_c, st_c = _ssd_scan(xbc_c, dtc_c, dtr_c, alog, skip_diag, h0, d_inner)
    y_l, _ = _ssd_scan(xbc_l, dtc_l, dtr_l, alog, skip_diag, st_c, d_inner)
    x = _ssd_out(y_l, zx_l, p["norm_w"], p["out_proj"], x, g1, pos)
    if with_ctx:
        ctx = _ssd_out(y_c, zx_c, p["norm_w"], p["out_proj"], ctx, cg1)
    return x, ctx


def _fourier_layer(x, ctx, mods, mods_c, nw, w_out, with_ctx):
    d = x.shape[-1]
    gw = d // N_FOURIER_GROUPS
    cc, sc_ = _dft_tables(gw, gw ** -0.5)
    w_cs = jnp.concatenate([cc, sc_], axis=1).astype(BF16)

    def mix(u, shift, scale, gate):
        l = u.shape[1]
        cl, sl = _dft_tables(l, l ** -0.5)
        a_mat = jnp.concatenate([cl, -sl], axis=1).astype(BF16)
        pq = _chan_dft(u, nw, shift, scale, w_cs)
        f = _seq_dft(a_mat, pq.reshape(u.shape[0], 2 * l, d))
        return _matmul_residual(f, w_out, u, gate)

    x = mix(x, mods[0], mods[1], mods[2])
    if with_ctx:
        ctx = mix(ctx, mods_c[0], mods_c[1], mods_c[2])
    return x, ctx


def kernel(x, c, ctx, c_ctx, w_mod, b_mod, norm_w, ffn_w_gate, ffn_w_up, ffn_w_down, ssd_in_proj, ssd_conv_w,
           ssd_conv_b, ssd_dt_bias, ssd_a_log, ssd_d, ssd_norm_w, ssd_out_proj, fourier_w_out, pool_w, pool_scale,
           final_norm_w):
    b, l, d = x.shape
    depth = w_mod.shape[0]
    n_ctx_rows = V7X_SUBLANES
    c_all = jnp.concatenate([c, jnp.broadcast_to(c_ctx[None, :], (n_ctx_rows, d))], axis=0)
    mod_all = _modulation(c_all, w_mod, b_mod)

    pos = _pos_tables(l, d)
    for i in range(depth):
        kind, j = i % N_MIXERS, i // N_MIXERS
        last = i == depth - 1
        need_ctx = (not last) or kind == 0
        mods = [mod_all[i, :b, None, k * d:(k + 1) * d] for k in range(6)]
        mods_c = [jnp.broadcast_to(mod_all[i, b:b + 1, None, k * d:(k + 1) * d], (b, 1, d)) for k in range(6)]
        nw1 = norm_w[i, 0][None, :]
        nw2 = norm_w[i, 1][None, :]
        if kind == 0:
            p = dict(in_proj=ssd_in_proj[j].astype(BF16), conv_w=ssd_conv_w[j], conv_b=ssd_conv_b[j][None, :],
                     dt_bias=ssd_dt_bias[j], a_log=ssd_a_log[j], d=ssd_d[j], norm_w=ssd_norm_w[j][None, :],
                     out_proj=ssd_out_proj[j].astype(BF16))
            x, ctx = _ssd_layer(x, ctx, mods, mods_c, nw1, p, not last, pos if i == 0 else None)
        elif kind == 1:
            x, ctx = _fourier_layer(x, ctx, mods, mods_c, nw1, fourier_w_out[j].astype(BF16), not last)
        else:
            pw = pool_w[j].astype(BF16)
            ps = pool_scale[j][None, :]
            x = _pool_mixer(x, nw1, mods[0], mods[1], mods[2], pw, ps)
            if not last:
                ctx = _pool_mixer(ctx, nw1, mods_c[0], mods_c[1], mods_c[2], pw, ps)
        wg, wu, wd = ffn_w_gate[i].astype(BF16), ffn_w_up[i].astype(BF16), ffn_w_down[i].astype(BF16)
        fw = final_norm_w[None, :] if last else None
        x = _ffn(x, nw2, mods[3], mods[4], mods[5], wg, wu, wd, fw)
        if not last:
            ctx = _ffn(ctx, nw2, mods_c[3], mods_c[4], mods_c[5], wg, wu, wd)
        del need_ctx
    return x
```

```python
import functools
import math

import jax
import jax.numpy as jnp
from jax import lax
from jax.experimental import pallas as pl
from jax.experimental.pallas import tpu as pltpu

F32 = jnp.float32
BF16 = jnp.bfloat16

GRID_WIDTH = 64
POS_BASE = 10000.0
NORM_EPS = 1e-6
SSD_NORM_EPS = 1e-5
SSD_HEAD_DIM = 64
SSD_GROUPS = 8
SSD_STATE = 128
SSD_CHUNK = 128
N_FOURIER_GROUPS = 4
POOL_WINDOWS = (2, 4, 8, 16)
N_MIXERS = 3
SCAN_CHUNKS_PER_STEP = 8

V7X_LANES = 128
V7X_SUBLANES = 8
V7X_VMEM_BYTES = 64 * 1024 * 1024
V7X_VMEM_LIMIT_BYTES = V7X_VMEM_BYTES - 8 * 1024 * 1024
HALO_ROWS = V7X_SUBLANES
NEG_BIG = -1e30
LOG2_E = 1.4426950408889634


def _cparams(semantics):
    return pltpu.CompilerParams(dimension_semantics=semantics,
                                vmem_limit_bytes=V7X_VMEM_LIMIT_BYTES)


def _tile(n, pref):
    t = min(n, pref)
    assert n % t == 0, (n, t)
    return t


def _silu(v):
    h = 0.5 * v
    return h + h * jnp.tanh(h)


def _ada_norm(x, nw, shift, scale):
    ms = jnp.mean(x * x, axis=-1, keepdims=True)
    y = x * lax.rsqrt(ms + NORM_EPS)
    return (y * nw) * (1.0 + scale) + shift


def _pos_tile(rt_ref, ct_ref, tile_idx, t):
    w = GRID_WIDTH
    n_rows = t // w
    half = rt_ref.shape[1]
    r0 = tile_idx * n_rows
    left = jnp.concatenate([jnp.broadcast_to(rt_ref[pl.ds(r0 + rr, 1), :], (w, half)) for rr in range(n_rows)],
                           axis=0)
    right = jnp.concatenate([ct_ref[...]] * n_rows, axis=0)
    return jnp.concatenate([left, right], axis=1)


def _split3(v):
    hi = v.astype(BF16)
    r1 = v - hi.astype(F32)
    mid = r1.astype(BF16)
    lo = (r1 - mid.astype(F32)).astype(BF16)
    return hi, mid, lo


def _dot(a, b):
    return jnp.dot(a, b, preferred_element_type=F32)


def _mod_kernel(c_ref, w_ref, b_ref, o_ref):
    s = _silu(c_ref[...])
    o_ref[...] = jnp.dot(s, w_ref[...], preferred_element_type=F32,
                         precision=lax.Precision.HIGHEST) + b_ref[...]


def _modulation(c_all, w_mod, b_mod):
    depth, d, n = w_mod.shape
    rows = c_all.shape[0]
    tn = _tile(n, 1024)
    return pl.pallas_call(
        _mod_kernel,
        grid=(depth, n // tn),
        in_specs=[
            pl.BlockSpec((rows, d), lambda i, j: (0, 0)),
            pl.BlockSpec((None, d, tn), lambda i, j: (i, 0, j)),
            pl.BlockSpec((None, 1, tn), lambda i, j: (i, 0, j)),
        ],
        out_specs=pl.BlockSpec((None, rows, tn), lambda i, j: (i, 0, j)),
        out_shape=jax.ShapeDtypeStruct((depth, rows, n), F32),
        compiler_params=_cparams(("parallel", "parallel")),
        name="modulation",
    )(c_all, w_mod, b_mod.reshape(depth, 1, n))


def _in_proj_kernel(x_ref, *rest, with_pos):
    if with_pos:
        rt_ref, ct_ref, nw_ref, sh_ref, sc_ref, w_ref, wdt_ref, b_ref, o_ref, dt_ref, h_ref = rest
    else:
        nw_ref, sh_ref, sc_ref, w_ref, wdt_ref, b_ref, o_ref, dt_ref, h_ref = rest

    @pl.when(pl.program_id(2) == 0)
    def _():
        xv = x_ref[...]
        if with_pos:
            xv = xv + _pos_tile(rt_ref, ct_ref, pl.program_id(1), xv.shape[0])
        h = _ada_norm(xv, nw_ref[...], sh_ref[...], sc_ref[...]).astype(BF16)
        h_ref[...] = h
        v = _dot(h, wdt_ref[...]) + b_ref[...]
        dt_ref[...] = jnp.maximum(v, 0.0) + jnp.log1p(jnp.exp(-jnp.abs(v)))

    o_ref[...] = _dot(h_ref[...], w_ref[...]).astype(o_ref.dtype)


def _in_proj(x, pos, nw, shift, scale, w, dt_bias):
    b, l, d = x.shape
    ndt = dt_bias.shape[1]
    n = w.shape[1] - ndt
    assert n % ndt == 0
    dt_block = n // ndt
    t = _tile(l, 1024)
    with_pos = pos is not None
    tn = _tile(n, 1024)
    pos_args = list(pos) if with_pos else []
    pos_specs = [pl.BlockSpec(a.shape, lambda i, j, k: (0, 0)) for a in pos_args]
    assert not with_pos or t % GRID_WIDTH == 0
    return pl.pallas_call(
        functools.partial(_in_proj_kernel, with_pos=with_pos),
        grid=(b, l // t, n // tn),
        in_specs=[pl.BlockSpec((None, t, d), lambda i, j, k: (i, j, 0))] + pos_specs + [
            pl.BlockSpec((1, d), lambda i, j, k: (0, 0)),
            pl.BlockSpec((None, 1, d), lambda i, j, k: (i, 0, 0)),
            pl.BlockSpec((None, 1, d), lambda i, j, k: (i, 0, 0)),
            pl.BlockSpec((d, tn), lambda i, j, k: (0, k)),
            pl.BlockSpec((d, ndt), lambda i, j, k: (0, dt_block)),
            pl.BlockSpec((1, ndt), lambda i, j, k: (0, 0)),
        ],
        out_specs=[pl.BlockSpec((None, t, tn), lambda i, j, k: (i, j, k)),
                   pl.BlockSpec((None, t, ndt), lambda i, j, k: (i, j, 0))],
        out_shape=[jax.ShapeDtypeStruct((b, l, n), BF16), jax.ShapeDtypeStruct((b, l, ndt), F32)],
        scratch_shapes=[pltpu.VMEM((t, d), BF16)],
        compiler_params=_cparams(("parallel", "parallel", "arbitrary")),
        name="in_proj",
    )(x, *pos_args, nw, shift, scale, w, w, dt_bias)


def _conv_kernel(cur_ref, prev_ref, next_ref, sel_ref, w_ref, b_ref, o_ref, *, t, n_tiles, width):
    l = pl.program_id(1)
    pad = width // 2
    halo = prev_ref.shape[0]
    sb = sel_ref.shape[1]
    zero = jnp.zeros(prev_ref.shape, prev_ref.dtype)
    prev = jnp.where(l > 0, prev_ref[...], zero)
    nxt = jnp.where(l < n_tiles - 1, next_ref[...], zero)
    ext = jnp.concatenate([prev, cur_ref[...], nxt], axis=0)
    for s in range(t // sb):
        win = ext[s * sb:(s + 1) * sb + 2 * halo]
        acc = b_ref[...] + w_ref[pad:pad + 1, :] * win[halo:halo + sb].astype(F32)
        for k in range(width):
            if k != pad:
                acc = acc + w_ref[k:k + 1, :] * _dot(sel_ref[k], win)
        o_ref[s * sb:(s + 1) * sb, :] = _silu(acc).astype(o_ref.dtype)


def _conv_silu(zx, conv_w, conv_b, col_offset):
    b, l, _ = zx.shape
    width, c = conv_w.shape
    pad = width // 2
    halo = 2 * V7X_SUBLANES
    assert pad <= halo
    t = _tile(l, 512)
    tc = _tile(c, 1024)
    sb = _tile(t, V7X_LANES)
    n_tiles = l // t
    off = col_offset // tc
    hb = t // halo
    last_hb = l // halo - 1
    i_idx = jnp.arange(sb)[None, :, None]
    j_idx = jnp.arange(sb + 2 * halo)[None, None, :]
    k_idx = jnp.arange(width)[:, None, None]
    sel = (j_idx == i_idx + halo + k_idx - pad).astype(BF16)
    return pl.pallas_call(
        functools.partial(_conv_kernel, t=t, n_tiles=n_tiles, width=width),
        grid=(b, n_tiles, c // tc),
        in_specs=[
            pl.BlockSpec((None, t, tc), lambda i, j, k: (i, j, off + k)),
            pl.BlockSpec((None, halo, tc), lambda i, j, k: (i, jnp.maximum(j * hb - 1, 0), off + k)),
            pl.BlockSpec((None, halo, tc), lambda i, j, k: (i, jnp.minimum((j + 1) * hb, last_hb), off + k)),
            pl.BlockSpec((width, sb, sb + 2 * halo), lambda i, j, k: (0, 0, 0)),
            pl.BlockSpec((width, tc), lambda i, j, k: (0, k)),
            pl.BlockSpec((1, tc), lambda i, j, k: (0, k)),
        ],
        out_specs=pl.BlockSpec((None, t, tc), lambda i, j, k: (i, j, k)),
        out_shape=jax.ShapeDtypeStruct((b, l, c), BF16),
        compiler_params=_cparams(("parallel", "parallel", "parallel")),
        name="conv_silu",
    )(zx, zx, zx, sel, conv_w, conv_b)


def _scan_kernel(xs_ref, b_ref, c_ref, dtc_ref, dtr_ref, alr_ref, alc_ref, skd_ref, h0_ref,
                 tm_ref, tmt_ref, madd_ref, y_ref, st_ref, *, cps, heads):
    d = pl.program_id(1)
    s = pl.program_id(3)
    q = SSD_CHUNK
    p = SSD_HEAD_DIM

    @pl.when(s == 0)
    def _():
        st_ref[...] = h0_ref[...]

    lane = lax.broadcasted_iota(jnp.int32, (q, 2 * p), 1)
    lo_half = lane < p
    tm = tm_ref[...]
    tm_t = tmt_ref[...]

    dtr = dtr_ref[...]
    la_c = dtc_ref[...] * (-jnp.exp(alr_ref[...]) * LOG2_E)
    la_r = dtr * (-jnp.exp(alc_ref[...]) * LOG2_E)
    c_hi, c_mid, c_lo = _split3(la_c)
    cum_c = _dot(tm, c_hi) + _dot(tm, c_mid) + _dot(tm, c_lo)
    r_hi, r_mid, r_lo = _split3(la_r)
    cum_r = _dot(r_hi, tm_t) + _dot(r_mid, tm_t) + _dot(r_lo, tm_t)
    tot_r = jnp.sum(la_r, axis=1, keepdims=True)
    ldt_r = jnp.log2(dtr)
    f_r = jnp.exp2(tot_r - cum_r + ldt_r)
    rowb_all = cum_r - ldt_r

    st = st_ref[...]
    for i in range(cps):
        j = i + d * (cps - 1 - 2 * i)
        off = pl.multiple_of(j * q, q)
        x16 = xs_ref[pl.ds(off, q), :]
        b16 = b_ref[pl.ds(off, q), :]
        c16 = c_ref[pl.ds(off, q), :]
        bt16 = b16.astype(F32).T.astype(BF16)
        scores16 = _dot(c16, bt16).astype(BF16)
        y_state = _dot(c16, st.astype(BF16))

        y_parts, upd_parts, ec_parts = [], [], []
        for pair in range(heads // 2):
            ms, bs, ecs = [], [], []
            for e in (2 * pair, 2 * pair + 1):
                k = i * heads + e
                colb = jnp.broadcast_to(cum_c[:, k:k + 1], (q, q))
                dec = jnp.exp2((colb - rowb_all[k:k + 1, :]) + madd_ref[...])
                ms.append(scores16 * dec.astype(BF16) + skd_ref[e])
                bs.append(bt16 * jnp.broadcast_to(f_r[k:k + 1, :], (q, q)).astype(BF16))
                ecs.append(jnp.exp2(colb))
            xp = x16[:, pair * 2 * p:(pair + 1) * 2 * p]
            zero = jnp.zeros_like(xp)
            rhs = jnp.concatenate([jnp.where(lo_half, xp, zero), jnp.where(lo_half, zero, xp)], axis=0)
            y_parts.append(_dot(jnp.concatenate(ms, axis=1), rhs))
            upd_parts.append(_dot(jnp.concatenate(bs, axis=1), rhs))
            ec_parts.append(jnp.where(lo_half, ecs[0], ecs[1]))
        ec = jnp.concatenate(ec_parts, axis=1)
        y = jnp.concatenate(y_parts, axis=1) + y_state * ec
        y_ref[pl.ds(off, q), :] = y.astype(y_ref.dtype)
        etot = jnp.where(d == 0, ec[q - 1:q, :], ec[0:1, :])
        st = st * etot + jnp.concatenate(upd_parts, axis=1)
    st_ref[...] = st


def _scan_masks():
    q = SSD_CHUNK
    row = jnp.arange(q)[:, None]
    col = jnp.arange(q)[None, :]
    tri = jnp.stack([row >= col, row <= col])
    return (tri.astype(BF16), jnp.swapaxes(tri, 1, 2).astype(BF16),
            jnp.where(tri, 0.0, NEG_BIG).astype(F32))


def _scan_chunks_per_step(l):
    return min(l // SSD_CHUNK, SCAN_CHUNKS_PER_STEP)


def _dt_layouts(dts, heads):
    b, l, _ = dts.shape
    cps = _scan_chunks_per_step(l)
    steps = l // (cps * SSD_CHUNK)
    v = dts.reshape(b, steps, cps, SSD_CHUNK, 2, SSD_GROUPS, heads)
    v = jnp.stack([v[:, :, :, :, 0], v[:, :, ::-1, :, 1]], axis=1)
    col = v.transpose(0, 1, 5, 2, 4, 3, 6).reshape(b, 2 * SSD_GROUPS, steps, SSD_CHUNK, cps * heads)
    row = v.transpose(0, 1, 5, 2, 3, 6, 4).reshape(b, 2 * SSD_GROUPS, steps, cps * heads, SSD_CHUNK)
    return col, row


def _ssd_scan(xbc, dtc5, dtr5, alog, skip_diag, h0, d_inner):
    b, l, _ = xbc.shape
    g = SSD_GROUPS
    n = SSD_STATE
    gw = d_inner // g
    heads = gw // SSD_HEAD_DIM
    cps = _scan_chunks_per_step(l)
    r = cps * SSD_CHUNK
    steps = l // r
    b_off = d_inner // n
    c_off = (d_inner + g * n) // n
    alog_t = jnp.tile(alog, (1, cps))
    alog_row = alog_t[:, None, :]
    alog_col = alog_t[:, :, None]

    def cb(si, di):
        return si + di * (steps - 1 - 2 * si)

    tm, tm_t, madd = _scan_masks()
    mask_spec = pl.BlockSpec((None, SSD_CHUNK, SSD_CHUNK), lambda bi, di, gi, si: (di, 0, 0))
    return pl.pallas_call(
        functools.partial(_scan_kernel, cps=cps, heads=heads),
        grid=(b, 2, g, steps),
        in_specs=[
            pl.BlockSpec((None, r, gw), lambda bi, di, gi, si: (bi, cb(si, di), gi)),
            pl.BlockSpec((None, r, n), lambda bi, di, gi, si: (bi, cb(si, di), b_off + gi)),
            pl.BlockSpec((None, r, n), lambda bi, di, gi, si: (bi, cb(si, di), c_off + gi)),
            pl.BlockSpec((None, None, None, SSD_CHUNK, cps * heads),
                         lambda bi, di, gi, si: (bi, di * g + gi, cb(si, di), 0, 0)),
            pl.BlockSpec((None, None, None, cps * heads, SSD_CHUNK),
                         lambda bi, di, gi, si: (bi, di * g + gi, cb(si, di), 0, 0)),
            pl.BlockSpec((None, 1, cps * heads), lambda bi, di, gi, si: (di * g + gi, 0, 0)),
            pl.BlockSpec((None, cps * heads, 1), lambda bi, di, gi, si: (di * g + gi, 0, 0)),
            pl.BlockSpec((None, heads, SSD_CHUNK, SSD_CHUNK), lambda bi, di, gi, si: (di * g + gi, 0, 0, 0)),
            pl.BlockSpec((None, None, None, n, gw), lambda bi, di, gi, si: (bi, di, gi, 0, 0)),
            mask_spec, mask_spec, mask_spec,
        ],
        out_specs=[
            pl.BlockSpec((None, None, r, gw), lambda bi, di, gi, si: (di, bi, cb(si, di), gi)),
            pl.BlockSpec((None, None, None, n, gw), lambda bi, di, gi, si: (bi, di, gi, 0, 0)),
        ],
        out_shape=[jax.ShapeDtypeStruct((2, b, l, d_inner), BF16),
                   jax.ShapeDtypeStruct((b, 2, g, n, gw), F32)],
        compiler_params=_cparams(("parallel", "parallel", "parallel", "arbitrary")),
        name="ssd_scan",
    )(xbc, xbc, xbc, dtc5, dtr5, alog_row, alog_col, skip_diag, h0, tm, tm_t, madd)


def _ssd_out_kernel(yf_ref, yb_ref, z_ref, nw_ref, w_ref, x_ref, g_ref, *rest, gw, with_pos):
    if with_pos:
        rt_ref, ct_ref, o_ref, a_ref = rest
    else:
        o_ref, a_ref = rest
    for gi in range(a_ref.shape[1] // gw):
        sl = slice(gi * gw, (gi + 1) * gw)
        v = (yf_ref[:, sl].astype(F32) + yb_ref[:, sl].astype(F32)) * _silu(z_ref[:, sl].astype(F32))
        ms = jnp.mean(v * v, axis=-1, keepdims=True)
        a_ref[:, sl] = (v * lax.rsqrt(ms + SSD_NORM_EPS) * nw_ref[:, sl]).astype(BF16)
    xv = x_ref[...]
    if with_pos:
        xv = xv + _pos_tile(rt_ref, ct_ref, pl.program_id(1), xv.shape[0])
    o_ref[...] = xv + g_ref[...] * _dot(a_ref[...], w_ref[...])


def _ssd_out(y2, zx, norm_w, w_out, x, gate, pos=None):
    b, l, d = x.shape
    k = w_out.shape[0]
    t = _tile(l, 256)
    with_pos = pos is not None
    pos_args = list(pos) if with_pos else []
    pos_specs = [pl.BlockSpec(a.shape, lambda i, j: (0, 0)) for a in pos_args]
    assert not with_pos or t % GRID_WIDTH == 0
    return pl.pallas_call(
        functools.partial(_ssd_out_kernel, gw=k // SSD_GROUPS, with_pos=with_pos),
        grid=(b, l // t),
        in_specs=[
            pl.BlockSpec((None, None, t, k), lambda i, j: (0, i, j, 0)),
            pl.BlockSpec((None, None, t, k), lambda i, j: (1, i, j, 0)),
            pl.BlockSpec((None, t, k), lambda i, j: (i, j, 0)),
            pl.BlockSpec((1, k), lambda i, j: (0, 0)),
            pl.BlockSpec((k, d), lambda i, j: (0, 0), pipeline_mode=pl.Buffered(1)),
            pl.BlockSpec((None, t, d), lambda i, j: (i, j, 0)),
            pl.BlockSpec((None, 1, d), lambda i, j: (i, 0, 0)),
        ] + pos_specs,
        out_specs=pl.BlockSpec((None, t, d), lambda i, j: (i, j, 0)),
        out_shape=jax.ShapeDtypeStruct(x.shape, F32),
        scratch_shapes=[pltpu.VMEM((t, k), BF16)],
        compiler_params=_cparams(("parallel", "parallel")),
        name="ssd_out",
    )(y2, y2, zx, norm_w, w_out, x, gate, *pos_args)


def _mm_res_kernel(a_ref, w_ref, x_ref, g_ref, o_ref):
    o_ref[...] = x_ref[...] + g_ref[...] * _dot(a_ref[...], w_ref[...])


def _matmul_residual(a, w, x, gate):
    b, l, d = x.shape
    k = w.shape[0]
    t = _tile(l, 1024)
    tn = _tile(d, 1024)
    return pl.pallas_call(
        _mm_res_kernel,
        grid=(b, l // t, d // tn),
        in_specs=[
            pl.BlockSpec((None, t, k), lambda i, j, n: (i, j, 0)),
            pl.BlockSpec((k, tn), lambda i, j, n: (0, n)),
            pl.BlockSpec((None, t, tn), lambda i, j, n: (i, j, n)),
            pl.BlockSpec((None, 1, tn), lambda i, j, n: (i, 0, n)),
        ],
        out_specs=pl.BlockSpec((None, t, tn), lambda i, j, n: (i, j, n)),
        out_shape=jax.ShapeDtypeStruct(x.shape, F32),
        compiler_params=_cparams(("parallel", "parallel", "parallel")),
        name="matmul_residual",
    )(a, w, x, gate)


def _ffn_kernel(x_ref, nw_ref, sh_ref, sc_ref, g_ref, wg_ref, wu_ref, wd_ref, *rest, final):
    if final:
        fw_ref, o_ref, h_ref = rest
    else:
        o_ref, h_ref = rest
    j = pl.program_id(2)

    @pl.when(j == 0)
    def _():
        h_ref[...] = _ada_norm(x_ref[...], nw_ref[...], sh_ref[...], sc_ref[...]).astype(BF16)
        o_ref[...] = jnp.zeros_like(o_ref)

    h = h_ref[...]
    a = _dot(h, wg_ref[...])
    u = _dot(h, wu_ref[...])
    o_ref[...] += _dot((_silu(a) * u).astype(BF16), wd_ref[...])

    @pl.when(j == pl.num_programs(2) - 1)
    def _():
        out = x_ref[...] + g_ref[...] * o_ref[...]
        if final:
            ms = jnp.mean(out * out, axis=-1, keepdims=True)
            out = out * lax.rsqrt(ms + NORM_EPS) * fw_ref[...]
        o_ref[...] = out


def _ffn(x, nw, shift, scale, gate, wg, wu, wd, final_w=None):
    b, l, d = x.shape
    hdim = wg.shape[1]
    t = _tile(l, 512)
    th = _tile(hdim, 512)
    final = final_w is not None
    vec = pl.BlockSpec((None, 1, d), lambda i, j, k: (i, 0, 0))
    in_specs = [
        pl.BlockSpec((None, t, d), lambda i, j, k: (i, j, 0)),
        pl.BlockSpec((1, d), lambda i, j, k: (0, 0)),
        vec, vec, vec,
        pl.BlockSpec((d, th), lambda i, j, k: (0, k)),
        pl.BlockSpec((d, th), lambda i, j, k: (0, k)),
        pl.BlockSpec((th, d), lambda i, j, k: (k, 0)),
    ]
    args = [x, nw, shift, scale, gate, wg, wu, wd]
    if final:
        in_specs.append(pl.BlockSpec((1, d), lambda i, j, k: (0, 0)))
        args.append(final_w)
    return pl.pallas_call(
        functools.partial(_ffn_kernel, final=final),
        grid=(b, l // t, hdim // th),
        in_specs=in_specs,
        out_specs=pl.BlockSpec((None, t, d), lambda i, j, k: (i, j, 0)),
        out_shape=jax.ShapeDtypeStruct(x.shape, F32),
        scratch_shapes=[pltpu.VMEM((t, d), BF16)],
        compiler_params=_cparams(("parallel", "parallel", "arbitrary")),
        name="ffn",
    )(*args)


def _chan_dft_kernel(x_ref, nw_ref, sh_ref, sc_ref, w_ref, o_ref, *, gw):
    h = _ada_norm(x_ref[...], nw_ref[...], sh_ref[...], sc_ref[...]).astype(BF16)
    for gi in range(h.shape[1] // gw):
        sl = slice(gi * gw, (gi + 1) * gw)
        r = _dot(h[:, sl], w_ref[...])
        o_ref[0, :, sl] = r[:, :gw].astype(BF16)
        o_ref[1, :, sl] = r[:, gw:].astype(BF16)


def _chan_dft(x, nw, shift, scale, w_cs):
    b, l, d = x.shape
    gw = w_cs.shape[0]
    t = _tile(l, 512)
    vec = pl.BlockSpec((None, 1, d), lambda i, j: (i, 0, 0))
    return pl.pallas_call(
        functools.partial(_chan_dft_kernel, gw=gw),
        grid=(b, l // t),
        in_specs=[
            pl.BlockSpec((None, t, d), lambda i, j: (i, j, 0)),
            pl.BlockSpec((1, d), lambda i, j: (0, 0)),
            vec, vec,
            pl.BlockSpec((gw, 2 * gw), lambda i, j: (0, 0)),
        ],
        out_specs=pl.BlockSpec((None, 2, t, d), lambda i, j: (i, 0, j, 0)),
        out_shape=jax.ShapeDtypeStruct((b, 2, l, d), BF16),
        compiler_params=_cparams(("parallel", "parallel")),
        name="chan_dft",
    )(x, nw, shift, scale, w_cs)


def _seq_dft_kernel(a_ref, pq_ref, o_ref, acc_ref):
    k = pl.program_id(3)

    @pl.when(k == 0)
    def _():
        acc_ref[...] = jnp.zeros_like(acc_ref)

    acc_ref[...] += _dot(a_ref[...], pq_ref[...])

    @pl.when(k == pl.num_programs(3) - 1)
    def _():
        o_ref[...] = acc_ref[...].astype(o_ref.dtype)


def _seq_dft(a_mat, pq):
    b, k2, d = pq.shape
    l = a_mat.shape[0]
    tm = _tile(l, 1024)
    tn = _tile(d, 1024)
    tk = _tile(k2, 2048)
    return pl.pallas_call(
        _seq_dft_kernel,
        grid=(b, l // tm, d // tn, k2 // tk),
        in_specs=[
            pl.BlockSpec((tm, tk), lambda i, m, n, k: (m, k)),
            pl.BlockSpec((None, tk, tn), lambda i, m, n, k: (i, k, n)),
        ],
        out_specs=pl.BlockSpec((None, tm, tn), lambda i, m, n, k: (i, m, n)),
        out_shape=jax.ShapeDtypeStruct((b, l, d), BF16),
        scratch_shapes=[pltpu.VMEM((tm, tn), F32)],
        compiler_params=_cparams(("parallel", "parallel", "parallel", "arbitrary")),
        name="seq_dft",
    )(a_mat, pq)


def _dft_tables(n, scale):
    idx = jnp.arange(n, dtype=jnp.int32)

    def cs(prod):
        ang = (prod % n).astype(F32) * (2.0 * math.pi / n)
        return jnp.cos(ang), jnp.sin(ang)

    m = GRID_WIDTH
    if n <= m or n % m:
        c, s = cs(idx[:, None] * idx[None, :])
        return c * scale, s * scale
    ca, sa = cs(idx[:, None] * (m * jnp.arange(n // m, dtype=jnp.int32))[None, :])
    cb, sb = cs(idx[:, None] * jnp.arange(m, dtype=jnp.int32)[None, :])
    ca, sa = ca[:, :, None] * scale, sa[:, :, None] * scale
    cb, sb = cb[:, None, :], sb[:, None, :]
    return (ca * cb - sa * sb).reshape(n, n), (sa * cb + ca * sb).reshape(n, n)


def _pool_kernel(cur_ref, prev_ref, next_ref, nw_ref, sh_ref, sc_ref, g_ref, w_ref, ps_ref,
                 o_ref, ext_ref, *, t, n_tiles, seq_len, gw):
    l = pl.program_id(1)
    nw, sh, sc = nw_ref[...], sh_ref[...], sc_ref[...]
    hc = _ada_norm(cur_ref[...], nw, sh, sc)
    ext_ref[0:HALO_ROWS, :] = jnp.where(l > 0, _ada_norm(prev_ref[...], nw, sh, sc), 0.0)
    ext_ref[HALO_ROWS:HALO_ROWS + t, :] = hc
    ext_ref[HALO_ROWS + t:, :] = jnp.where(l < n_tiles - 1, _ada_norm(next_ref[...], nw, sh, sc), 0.0)
    pos = l * t + lax.broadcasted_iota(jnp.int32, (t, 1), 0)
    for gi, win in enumerate(POOL_WINDOWS):
        sl = slice(gi * gw, (gi + 1) * gw)
        half = win // 2
        tot = ext_ref[pl.ds(HALO_ROWS - half, t), sl]
        for k in range(-half + 1, win - half):
            tot = tot + ext_ref[pl.ds(HALO_ROWS + k, t), sl]
        cnt = (jnp.minimum(pos + (win - half), seq_len) - jnp.maximum(pos - half, 0)).astype(F32)
        pooled = (tot / cnt - hc[:, sl]).astype(BF16)
        out = _dot(pooled, w_ref[gi]) * ps_ref[:, sl]
        o_ref[:, sl] = cur_ref[:, sl] + g_ref[:, sl] * out


def _pool_mixer(x, nw, shift, scale, gate, w_grp, pscale):
    b, l, d = x.shape
    ng, gw, _ = w_grp.shape
    assert max(POOL_WINDOWS) // 2 <= HALO_ROWS
    t = _tile(l, 512)
    n_tiles = l // t
    rb = t // HALO_ROWS
    last_rb = l // HALO_ROWS - 1
    vec = pl.BlockSpec((None, 1, d), lambda i, j: (i, 0, 0))
    return pl.pallas_call(
        functools.partial(_pool_kernel, t=t, n_tiles=n_tiles, seq_len=l, gw=gw),
        grid=(b, n_tiles),
        in_specs=[
            pl.BlockSpec((None, t, d), lambda i, j: (i, j, 0)),
            pl.BlockSpec((None, HALO_ROWS, d), lambda i, j: (i, jnp.maximum(j * rb - 1, 0), 0)),
            pl.BlockSpec((None, HALO_ROWS, d), lambda i, j: (i, jnp.minimum((j + 1) * rb, last_rb), 0)),
            pl.BlockSpec((1, d), lambda i, j: (0, 0)),
            vec, vec, vec,
            pl.BlockSpec((ng, gw, gw), lambda i, j: (0, 0, 0)),
            pl.BlockSpec((1, d), lambda i, j: (0, 0)),
        ],
        out_specs=pl.BlockSpec((None, t, d), lambda i, j: (i, j, 0)),
        out_shape=jax.ShapeDtypeStruct(x.shape, F32),
        scratch_shapes=[pltpu.VMEM((t + 2 * HALO_ROWS, d), F32)],
        compiler_params=_cparams(("parallel", "parallel")),
        name="pool_mixer",
    )(x, x, x, nw, shift, scale, gate, w_grp, pscale)


def _pos_tables(n_tokens, d):
    rows = n_tokens // GRID_WIDTH
    quarter = d // 4
    omega = 1.0 / (POS_BASE ** (jnp.arange(quarter, dtype=F32) / quarter))

    def table(n):
        v = jnp.arange(n, dtype=F32)[:, None] * omega
        return jnp.concatenate([jnp.sin(v), jnp.cos(v)], axis=-1)

    return table(rows), table(GRID_WIDTH)


def _flat(u):
    return u.reshape((1, u.shape[0] * u.shape[1]) + u.shape[2:])


def _unflat(u, b):
    return u.reshape((b, u.shape[0] * u.shape[1] // b) + u.shape[2:])


def _ssd_layer(x, ctx, mods, mods_c, nw, p, with_ctx, pos=None):
    sh1, sc1, g1 = mods[0], mods[1], mods[2]
    ch1, cs1, cg1 = mods_c[0], mods_c[1], mods_c[2]
    d_inner = p["out_proj"].shape[0]
    heads = p["dt_bias"].shape[-1]
    hpg = heads // SSD_GROUPS
    gw = d_inner // SSD_GROUPS
    dt_bias = p["dt_bias"].reshape(1, 2 * heads)
    alog = p["a_log"].reshape(2 * SSD_GROUPS, hpg)
    skip_diag = (p["d"].reshape(2 * SSD_GROUPS, hpg)[:, :, None, None]
               * jnp.eye(SSD_CHUNK, dtype=F32)[None, None]).astype(BF16)

    b = x.shape[0]

    def project(u, shift, scale, u_pos=None):
        zx, dts = _in_proj(u, u_pos, nw, shift, scale, p["in_proj"], dt_bias)
        xbc = _conv_silu(_unflat(zx, b), p["conv_w"], p["conv_b"], d_inner)
        dtc5, dtr5 = _dt_layouts(_unflat(dts, b), hpg)
        return zx, xbc, dtc5, dtr5

    zx_c, xbc_c, dtc_c, dtr_c = project(_flat(ctx), ch1, cs1)
    zx_l, xbc_l, dtc_l, dtr_l = project(x, sh1, sc1, pos)
    h0 = jnp.zeros((b, 2, SSD_GROUPS, SSD_STATE, gw), F32)
    y_c, st_c = _ssd_scan(xbc_c, dtc_c, dtr_c, alog, skip_diag, h0, d_inner)
    y_l, _ = _ssd_scan(xbc_l, dtc_l, dtr_l, alog, skip_diag, st_c, d_inner)
    x = _ssd_out(y_l, zx_l, p["norm_w"], p["out_proj"], x, g1, pos)
    if with_ctx:
        y_flat = y_c.reshape(2, 1, -1, d_inner)
        ctx = _unflat(_ssd_out(y_flat, zx_c, p["norm_w"], p["out_proj"], _flat(ctx), cg1), b)
    return x, ctx


def _fourier_layer(x, ctx, mods, mods_c, nw, w_out, with_ctx):
    d = x.shape[-1]
    gw = d // N_FOURIER_GROUPS
    cc, sc_ = _dft_tables(gw, gw ** -0.5)
    w_cs = jnp.concatenate([cc, sc_], axis=1).astype(BF16)

    def mix(u, shift, scale, gate):
        l = u.shape[1]
        cl, sl = _dft_tables(l, l ** -0.5)
        a_mat = jnp.concatenate([cl, -sl], axis=1).astype(BF16)
        pq = _chan_dft(u, nw, shift, scale, w_cs)
        f = _seq_dft(a_mat, pq.reshape(u.shape[0], 2 * l, d))
        if gate.shape[0] == 1:
            return _unflat(_matmul_residual(_flat(f), w_out, _flat(u), gate), u.shape[0])
        return _matmul_residual(f, w_out, u, gate)

    x = mix(x, mods[0], mods[1], mods[2])
    if with_ctx:
        b = ctx.shape[0]
        ctx = mix(ctx, jnp.broadcast_to(mods_c[0], (b, 1, d)), jnp.broadcast_to(mods_c[1], (b, 1, d)), mods_c[2])
    return x, ctx


def kernel(x, c, ctx, c_ctx, w_mod, b_mod, norm_w, ffn_w_gate, ffn_w_up, ffn_w_down, ssd_in_proj, ssd_conv_w,
           ssd_conv_b, ssd_dt_bias, ssd_a_log, ssd_d, ssd_norm_w, ssd_out_proj, fourier_w_out, pool_w, pool_scale,
           final_norm_w):
    b, l, d = x.shape
    depth = w_mod.shape[0]
    n_ctx_rows = V7X_SUBLANES
    c_all = jnp.concatenate([c, jnp.broadcast_to(c_ctx[None, :], (n_ctx_rows, d))], axis=0)
    mod_all = _modulation(c_all, w_mod, b_mod)

    pos = _pos_tables(l, d)
    for i in range(depth):
        kind, j = i % N_MIXERS, i // N_MIXERS
        last = i == depth - 1
        mods = [mod_all[i, :b, None, k * d:(k + 1) * d] for k in range(6)]
        mods_c = [mod_all[i, b:b + 1, None, k * d:(k + 1) * d] for k in range(6)]
        nw1 = norm_w[i, 0][None, :]
        nw2 = norm_w[i, 1][None, :]
        if kind == 0:
            p = dict(in_proj=ssd_in_proj[j].astype(BF16), conv_w=ssd_conv_w[j], conv_b=ssd_conv_b[j][None, :],
                     dt_bias=ssd_dt_bias[j], a_log=ssd_a_log[j], d=ssd_d[j], norm_w=ssd_norm_w[j][None, :],
                     out_proj=ssd_out_proj[j].astype(BF16))
            x, ctx = _ssd_layer(x, ctx, mods, mods_c, nw1, p, not last, pos if i == 0 else None)
        elif kind == 1:
            x, ctx = _fourier_layer(x, ctx, mods, mods_c, nw1, fourier_w_out[j].astype(BF16), not last)
        else:
            pw = pool_w[j].astype(BF16)
            ps = pool_scale[j][None, :]
            x = _pool_mixer(x, nw1, mods[0], mods[1], mods[2], pw, ps)
            if not last:
                ctx = _pool_mixer(ctx, nw1, *[jnp.broadcast_to(m, (b, 1, d)) for m in mods_c[:3]], pw, ps)
        wg, wu, wd = ffn_w_gate[i].astype(BF16), ffn_w_up[i].astype(BF16), ffn_w_down[i].astype(BF16)
        fw = final_norm_w[None, :] if last else None
        x = _ffn(x, nw2, mods[3], mods[4], mods[5], wg, wu, wd, fw)
        if not last:
            ctx = _unflat(_ffn(_flat(ctx), nw2, mods_c[3], mods_c[4], mods_c[5], wg, wu, wd), b)
    return x
```

```python
import functools
import math

import jax
import jax.numpy as jnp
from jax import lax
from jax.experimental import pallas as pl
from jax.experimental.pallas import tpu as pltpu

F32 = jnp.float32
BF16 = jnp.bfloat16

GRID_WIDTH = 64
POS_BASE = 10000.0
NORM_EPS = 1e-6
SSD_NORM_EPS = 1e-5
SSD_HEAD_DIM = 64
SSD_GROUPS = 8
SSD_STATE = 128
SSD_CHUNK = 128
N_FOURIER_GROUPS = 4
POOL_WINDOWS = (2, 4, 8, 16)
N_MIXERS = 3
SCAN_CHUNKS_PER_STEP = 8

V7X_LANES = 128
V7X_SUBLANES = 8
V7X_VMEM_BYTES = 64 * 1024 * 1024
V7X_VMEM_LIMIT_BYTES = V7X_VMEM_BYTES - 8 * 1024 * 1024
HALO_ROWS = V7X_SUBLANES
NEG_BIG = -1e30
LOG2_E = 1.4426950408889634


def _cparams(semantics):
    return pltpu.CompilerParams(dimension_semantics=semantics,
                                vmem_limit_bytes=V7X_VMEM_LIMIT_BYTES)


def _tile(n, pref):
    t = min(n, pref)
    assert n % t == 0, (n, t)
    return t


def _silu(v):
    h = 0.5 * v
    return h + h * jnp.tanh(h)


def _ada_norm(x, nw, shift, scale):
    ms = jnp.mean(x * x, axis=-1, keepdims=True)
    y = x * lax.rsqrt(ms + NORM_EPS)
    return (y * nw) * (1.0 + scale) + shift


def _pos_tile(rt_ref, ct_ref, tile_idx, t):
    w = GRID_WIDTH
    n_rows = t // w
    half = rt_ref.shape[1]
    r0 = tile_idx * n_rows
    left = jnp.concatenate([jnp.broadcast_to(rt_ref[pl.ds(r0 + rr, 1), :], (w, half)) for rr in range(n_rows)],
                           axis=0)
    right = jnp.concatenate([ct_ref[...]] * n_rows, axis=0)
    return jnp.concatenate([left, right], axis=1)


def _split3(v):
    hi = v.astype(BF16)
    r1 = v - hi.astype(F32)
    mid = r1.astype(BF16)
    lo = (r1 - mid.astype(F32)).astype(BF16)
    return hi, mid, lo


def _dot(a, b):
    return jnp.dot(a, b, preferred_element_type=F32)


def _mod_kernel(c_ref, w_ref, b_ref, o_ref):
    s = _silu(c_ref[...])
    o_ref[...] = jnp.dot(s, w_ref[...], preferred_element_type=F32,
                         precision=lax.Precision.HIGHEST) + b_ref[...]


def _modulation(c_all, w_mod, b_mod):
    depth, d, n = w_mod.shape
    rows = c_all.shape[0]
    tn = _tile(n, 1024)
    return pl.pallas_call(
        _mod_kernel,
        grid=(depth, n // tn),
        in_specs=[
            pl.BlockSpec((rows, d), lambda i, j: (0, 0)),
            pl.BlockSpec((None, d, tn), lambda i, j: (i, 0, j)),
            pl.BlockSpec((None, 1, tn), lambda i, j: (i, 0, j)),
        ],
        out_specs=pl.BlockSpec((None, rows, tn), lambda i, j: (i, 0, j)),
        out_shape=jax.ShapeDtypeStruct((depth, rows, n), F32),
        compiler_params=_cparams(("parallel", "parallel")),
        name="modulation",
    )(c_all, w_mod, b_mod.reshape(depth, 1, n))


def _in_proj_kernel(x_ref, *rest, with_pos):
    if with_pos:
        rt_ref, ct_ref, nw_ref, sh_ref, sc_ref, w_ref, wdt_ref, b_ref, o_ref, dt_ref, h_ref = rest
    else:
        nw_ref, sh_ref, sc_ref, w_ref, wdt_ref, b_ref, o_ref, dt_ref, h_ref = rest

    @pl.when(pl.program_id(2) == 0)
    def _():
        xv = x_ref[...]
        if with_pos:
            xv = xv + _pos_tile(rt_ref, ct_ref, pl.program_id(1), xv.shape[0])
        h = _ada_norm(xv, nw_ref[...], sh_ref[...], sc_ref[...]).astype(BF16)
        h_ref[...] = h
        v = _dot(h, wdt_ref[...]) + b_ref[...]
        dt_ref[...] = jnp.maximum(v, 0.0) + jnp.log1p(jnp.exp(-jnp.abs(v)))

    o_ref[...] = _dot(h_ref[...], w_ref[...]).astype(o_ref.dtype)


def _in_proj(x, pos, nw, shift, scale, w, layer, dt_bias):
    b, l, d = x.shape
    ndt = dt_bias.shape[1]
    n = w.shape[2] - ndt
    assert n % ndt == 0
    dt_block = n // ndt
    t = _tile(l, 1024)
    with_pos = pos is not None
    tn = _tile(n, 1024)
    pos_args = list(pos) if with_pos else []
    pos_specs = [pl.BlockSpec(a.shape, lambda i, j, k: (0, 0)) for a in pos_args]
    assert not with_pos or t % GRID_WIDTH == 0
    return pl.pallas_call(
        functools.partial(_in_proj_kernel, with_pos=with_pos),
        grid=(b, l // t, n // tn),
        in_specs=[pl.BlockSpec((None, t, d), lambda i, j, k: (i, j, 0))] + pos_specs + [
            pl.BlockSpec((1, d), lambda i, j, k: (0, 0)),
            pl.BlockSpec((None, 1, d), lambda i, j, k: (i, 0, 0)),
            pl.BlockSpec((None, 1, d), lambda i, j, k: (i, 0, 0)),
            pl.BlockSpec((None, d, tn), lambda i, j, k: (layer, 0, k)),
            pl.BlockSpec((None, d, ndt), lambda i, j, k: (layer, 0, dt_block)),
            pl.BlockSpec((1, ndt), lambda i, j, k: (0, 0)),
        ],
        out_specs=[pl.BlockSpec((None, t, tn), lambda i, j, k: (i, j, k)),
                   pl.BlockSpec((None, t, ndt), lambda i, j, k: (i, j, 0))],
        out_shape=[jax.ShapeDtypeStruct((b, l, n), BF16), jax.ShapeDtypeStruct((b, l, ndt), F32)],
        scratch_shapes=[pltpu.VMEM((t, d), BF16)],
        compiler_params=_cparams(("parallel", "parallel", "arbitrary")),
        name="in_proj",
    )(x, *pos_args, nw, shift, scale, w, w, dt_bias)


def _conv_kernel(cur_ref, prev_ref, next_ref, sel_ref, w_ref, b_ref, o_ref, *, t, n_tiles, width):
    l = pl.program_id(1)
    pad = width // 2
    halo = prev_ref.shape[0]
    sb = sel_ref.shape[1]
    zero = jnp.zeros(prev_ref.shape, prev_ref.dtype)
    prev = jnp.where(l > 0, prev_ref[...], zero)
    nxt = jnp.where(l < n_tiles - 1, next_ref[...], zero)
    ext = jnp.concatenate([prev, cur_ref[...], nxt], axis=0)
    for s in range(t // sb):
        win = ext[s * sb:(s + 1) * sb + 2 * halo]
        acc = b_ref[...] + w_ref[pad:pad + 1, :] * win[halo:halo + sb].astype(F32)
        for k in range(width):
            if k != pad:
                acc = acc + w_ref[k:k + 1, :] * _dot(sel_ref[k], win)
        o_ref[s * sb:(s + 1) * sb, :] = _silu(acc).astype(o_ref.dtype)


def _conv_silu(zx, conv_w, conv_b, col_offset):
    b, l, _ = zx.shape
    width, c = conv_w.shape
    pad = width // 2
    halo = 2 * V7X_SUBLANES
    assert pad <= halo
    t = _tile(l, 512)
    tc = _tile(c, 2048)
    sb = _tile(t, V7X_LANES)
    n_tiles = l // t
    off = col_offset // tc
    hb = t // halo
    last_hb = l // halo - 1
    i_idx = jnp.arange(sb)[None, :, None]
    j_idx = jnp.arange(sb + 2 * halo)[None, None, :]
    k_idx = jnp.arange(width)[:, None, None]
    sel = (j_idx == i_idx + halo + k_idx - pad).astype(BF16)
    return pl.pallas_call(
        functools.partial(_conv_kernel, t=t, n_tiles=n_tiles, width=width),
        grid=(b, n_tiles, c // tc),
        in_specs=[
            pl.BlockSpec((None, t, tc), lambda i, j, k: (i, j, off + k)),
            pl.BlockSpec((None, halo, tc), lambda i, j, k: (i, jnp.maximum(j * hb - 1, 0), off + k)),
            pl.BlockSpec((None, halo, tc), lambda i, j, k: (i, jnp.minimum((j + 1) * hb, last_hb), off + k)),
            pl.BlockSpec((width, sb, sb + 2 * halo), lambda i, j, k: (0, 0, 0)),
            pl.BlockSpec((width, tc), lambda i, j, k: (0, k)),
            pl.BlockSpec((1, tc), lambda i, j, k: (0, k)),
        ],
        out_specs=pl.BlockSpec((None, t, tc), lambda i, j, k: (i, j, k)),
        out_shape=jax.ShapeDtypeStruct((b, l, c), BF16),
        compiler_params=_cparams(("parallel", "parallel", "parallel")),
        name="conv_silu",
    )(zx, zx, zx, sel, conv_w, conv_b)


def _scan_kernel(xs_ref, b_ref, c_ref, dtc_ref, dtr_ref, alr_ref, alc_ref, skd_ref, h0_ref,
                 tm_ref, tmt_ref, madd_ref, y_ref, st_ref, *, cps, heads):
    d = pl.program_id(1)
    s = pl.program_id(3)
    q = SSD_CHUNK
    p = SSD_HEAD_DIM

    @pl.when(s == 0)
    def _():
        st_ref[...] = h0_ref[...]

    lane = lax.broadcasted_iota(jnp.int32, (q, 2 * p), 1)
    lo_half = lane < p
    tm = tm_ref[...]
    tm_t = tmt_ref[...]

    dtr = dtr_ref[...]
    la_c = dtc_ref[...] * (-jnp.exp(alr_ref[...]) * LOG2_E)
    la_r = dtr * (-jnp.exp(alc_ref[...]) * LOG2_E)
    c_hi, c_mid, c_lo = _split3(la_c)
    cum_c = _dot(tm, c_hi) + _dot(tm, c_mid) + _dot(tm, c_lo)
    r_hi, r_mid, r_lo = _split3(la_r)
    cum_r = _dot(r_hi, tm_t) + _dot(r_mid, tm_t) + _dot(r_lo, tm_t)
    tot_r = jnp.sum(la_r, axis=1, keepdims=True)
    ldt_r = jnp.log2(dtr)
    f_r = jnp.exp2(tot_r - cum_r + ldt_r)
    rowb_all = cum_r - ldt_r

    st = st_ref[...]
    for i in range(cps):
        j = i + d * (cps - 1 - 2 * i)
        off = pl.multiple_of(j * q, q)
        x16 = xs_ref[pl.ds(off, q), :]
        b16 = b_ref[pl.ds(off, q), :]
        c16 = c_ref[pl.ds(off, q), :]
        bt16 = b16.astype(F32).T.astype(BF16)
        scores16 = _dot(c16, bt16).astype(BF16)
        y_state = _dot(c16, st.astype(BF16))

        y_parts, upd_parts, ec_parts = [], [], []
        for pair in range(heads // 2):
            ms, bs, ecs = [], [], []
            for e in (2 * pair, 2 * pair + 1):
                k = i * heads + e
                colb = jnp.broadcast_to(cum_c[:, k:k + 1], (q, q))
                dec = jnp.exp2((colb - rowb_all[k:k + 1, :]) + madd_ref[...])
                ms.append(scores16 * dec.astype(BF16) + skd_ref[e])
                bs.append(bt16 * jnp.broadcast_to(f_r[k:k + 1, :], (q, q)).astype(BF16))
                ecs.append(jnp.exp2(colb))
            xp = x16[:, pair * 2 * p:(pair + 1) * 2 * p]
            zero = jnp.zeros_like(xp)
            rhs = jnp.concatenate([jnp.where(lo_half, xp, zero), jnp.where(lo_half, zero, xp)], axis=0)
            y_parts.append(_dot(jnp.concatenate(ms, axis=1), rhs))
            upd_parts.append(_dot(jnp.concatenate(bs, axis=1), rhs))
            ec_parts.append(jnp.where(lo_half, ecs[0], ecs[1]))
        ec = jnp.concatenate(ec_parts, axis=1)
        y = jnp.concatenate(y_parts, axis=1) + y_state * ec
        y_ref[pl.ds(off, q), :] = y.astype(y_ref.dtype)
        etot = jnp.where(d == 0, ec[q - 1:q, :], ec[0:1, :])
        st = st * etot + jnp.concatenate(upd_parts, axis=1)
    st_ref[...] = st


def _scan_masks():
    q = SSD_CHUNK
    row = jnp.arange(q)[:, None]
    col = jnp.arange(q)[None, :]
    tri = jnp.stack([row >= col, row <= col])
    return (tri.astype(BF16), jnp.swapaxes(tri, 1, 2).astype(BF16),
            jnp.where(tri, 0.0, NEG_BIG).astype(F32))


def _scan_chunks_per_step(l):
    return min(l // SSD_CHUNK, SCAN_CHUNKS_PER_STEP)


def _dt_layouts(dts, heads):
    b, l, _ = dts.shape
    cps = _scan_chunks_per_step(l)
    steps = l // (cps * SSD_CHUNK)
    v = dts.reshape(b, steps, cps, SSD_CHUNK, 2, SSD_GROUPS, heads)
    v = jnp.stack([v[:, :, :, :, 0], v[:, :, ::-1, :, 1]], axis=1)
    col = v.transpose(0, 1, 5, 2, 4, 3, 6).reshape(b, 2 * SSD_GROUPS, steps, SSD_CHUNK, cps * heads)
    row = v.transpose(0, 1, 5, 2, 3, 6, 4).reshape(b, 2 * SSD_GROUPS, steps, cps * heads, SSD_CHUNK)
    return col, row


def _ssd_scan(xbc, dtc5, dtr5, alog, skip_diag, h0, d_inner):
    b, l, _ = xbc.shape
    g = SSD_GROUPS
    n = SSD_STATE
    gw = d_inner // g
    heads = gw // SSD_HEAD_DIM
    cps = _scan_chunks_per_step(l)
    r = cps * SSD_CHUNK
    steps = l // r
    b_off = d_inner // n
    c_off = (d_inner + g * n) // n
    alog_t = jnp.tile(alog, (1, cps))
    alog_row = alog_t[:, None, :]
    alog_col = alog_t[:, :, None]

    def cb(si, di):
        return si + di * (steps - 1 - 2 * si)

    tm, tm_t, madd = _scan_masks()
    mask_spec = pl.BlockSpec((None, SSD_CHUNK, SSD_CHUNK), lambda bi, di, gi, si: (di, 0, 0))
    return pl.pallas_call(
        functools.partial(_scan_kernel, cps=cps, heads=heads),
        grid=(b, 2, g, steps),
        in_specs=[
            pl.BlockSpec((None, r, gw), lambda bi, di, gi, si: (bi, cb(si, di), gi)),
            pl.BlockSpec((None, r, n), lambda bi, di, gi, si: (bi, cb(si, di), b_off + gi)),
            pl.BlockSpec((None, r, n), lambda bi, di, gi, si: (bi, cb(si, di), c_off + gi)),
            pl.BlockSpec((None, None, None, SSD_CHUNK, cps * heads),
                         lambda bi, di, gi, si: (bi, di * g + gi, cb(si, di), 0, 0)),
            pl.BlockSpec((None, None, None, cps * heads, SSD_CHUNK),
                         lambda bi, di, gi, si: (bi, di * g + gi, cb(si, di), 0, 0)),
            pl.BlockSpec((None, 1, cps * heads), lambda bi, di, gi, si: (di * g + gi, 0, 0)),
            pl.BlockSpec((None, cps * heads, 1), lambda bi, di, gi, si: (di * g + gi, 0, 0)),
            pl.BlockSpec((None, heads, SSD_CHUNK, SSD_CHUNK), lambda bi, di, gi, si: (di * g + gi, 0, 0, 0)),
            pl.BlockSpec((None, None, None, n, gw), lambda bi, di, gi, si: (bi, di, gi, 0, 0)),
            mask_spec, mask_spec, mask_spec,
        ],
        out_specs=[
            pl.BlockSpec((None, None, r, gw), lambda bi, di, gi, si: (di, bi, cb(si, di), gi)),
            pl.BlockSpec((None, None, None, n, gw), lambda bi, di, gi, si: (bi, di, gi, 0, 0)),
        ],
        out_shape=[jax.ShapeDtypeStruct((2, b, l, d_inner), BF16),
                   jax.ShapeDtypeStruct((b, 2, g, n, gw), F32)],
        compiler_params=_cparams(("parallel", "parallel", "parallel", "arbitrary")),
        name="ssd_scan",
    )(xbc, xbc, xbc, dtc5, dtr5, alog_row, alog_col, skip_diag, h0, tm, tm_t, madd)


def _ssd_out_kernel(yf_ref, yb_ref, z_ref, nw_ref, w_ref, x_ref, g_ref, *rest, gw, with_pos):
    if with_pos:
        rt_ref, ct_ref, o_ref, a_ref = rest
    else:
        o_ref, a_ref = rest
    for gi in range(a_ref.shape[1] // gw):
        sl = slice(gi * gw, (gi + 1) * gw)
        v = (yf_ref[:, sl].astype(F32) + yb_ref[:, sl].astype(F32)) * _silu(z_ref[:, sl].astype(F32))
        ms = jnp.mean(v * v, axis=-1, keepdims=True)
        a_ref[:, sl] = (v * lax.rsqrt(ms + SSD_NORM_EPS) * nw_ref[:, sl]).astype(BF16)
    xv = x_ref[...]
    if with_pos:
        xv = xv + _pos_tile(rt_ref, ct_ref, pl.program_id(1), xv.shape[0])
    o_ref[...] = xv + g_ref[...] * _dot(a_ref[...], w_ref[...])


def _ssd_out(y2, zx, norm_w, w_out, layer, x, gate, pos=None):
    b, l, d = x.shape
    k = w_out.shape[1]
    t = _tile(l, 256)
    with_pos = pos is not None
    pos_args = list(pos) if with_pos else []
    pos_specs = [pl.BlockSpec(a.shape, lambda i, j: (0, 0)) for a in pos_args]
    assert not with_pos or t % GRID_WIDTH == 0
    return pl.pallas_call(
        functools.partial(_ssd_out_kernel, gw=k // SSD_GROUPS, with_pos=with_pos),
        grid=(b, l // t),
        in_specs=[
            pl.BlockSpec((None, None, t, k), lambda i, j: (0, i, j, 0)),
            pl.BlockSpec((None, None, t, k), lambda i, j: (1, i, j, 0)),
            pl.BlockSpec((None, t, k), lambda i, j: (i, j, 0)),
            pl.BlockSpec((1, k), lambda i, j: (0, 0)),
            pl.BlockSpec((None, k, d), lambda i, j: (layer, 0, 0), pipeline_mode=pl.Buffered(1)),
            pl.BlockSpec((None, t, d), lambda i, j: (i, j, 0)),
            pl.BlockSpec((None, 1, d), lambda i, j: (i, 0, 0)),
        ] + pos_specs,
        out_specs=pl.BlockSpec((None, t, d), lambda i, j: (i, j, 0)),
        out_shape=jax.ShapeDtypeStruct(x.shape, F32),
        scratch_shapes=[pltpu.VMEM((t, k), BF16)],
        compiler_params=_cparams(("parallel", "parallel")),
        name="ssd_out",
    )(y2, y2, zx, norm_w, w_out, x, gate, *pos_args)


def _mm_res_kernel(a_ref, w_ref, x_ref, g_ref, o_ref):
    o_ref[...] = x_ref[...] + g_ref[...] * _dot(a_ref[...], w_ref[...])


def _matmul_residual(a, w, x, gate):
    b, l, d = x.shape
    k = w.shape[0]
    t = _tile(l, 1024)
    tn = _tile(d, 1024)
    return pl.pallas_call(
        _mm_res_kernel,
        grid=(b, l // t, d // tn),
        in_specs=[
            pl.BlockSpec((None, t, k), lambda i, j, n: (i, j, 0)),
            pl.BlockSpec((k, tn), lambda i, j, n: (0, n)),
            pl.BlockSpec((None, t, tn), lambda i, j, n: (i, j, n)),
            pl.BlockSpec((None, 1, tn), lambda i, j, n: (i, 0, n)),
        ],
        out_specs=pl.BlockSpec((None, t, tn), lambda i, j, n: (i, j, n)),
        out_shape=jax.ShapeDtypeStruct(x.shape, F32),
        compiler_params=_cparams(("parallel", "parallel", "parallel")),
        name="matmul_residual",
    )(a, w, x, gate)


def _ffn_kernel(x_ref, nw_ref, sh_ref, sc_ref, g_ref, wg_ref, wu_ref, wd_ref, *rest, final):
    if final:
        fw_ref, o_ref, h_ref = rest
    else:
        o_ref, h_ref = rest
    j = pl.program_id(2)

    @pl.when(j == 0)
    def _():
        h_ref[...] = _ada_norm(x_ref[...], nw_ref[...], sh_ref[...], sc_ref[...]).astype(BF16)
        o_ref[...] = jnp.zeros_like(o_ref)

    h = h_ref[...]
    a = _dot(h, wg_ref[...])
    u = _dot(h, wu_ref[...])
    o_ref[...] += _dot((_silu(a) * u).astype(BF16), wd_ref[...])

    @pl.when(j == pl.num_programs(2) - 1)
    def _():
        out = x_ref[...] + g_ref[...] * o_ref[...]
        if final:
            ms = jnp.mean(out * out, axis=-1, keepdims=True)
            out = out * lax.rsqrt(ms + NORM_EPS) * fw_ref[...]
        o_ref[...] = out


def _ffn(x, nw, shift, scale, gate, wg, wu, wd, layer, final_w=None):
    b, l, d = x.shape
    hdim = wg.shape[2]
    t = _tile(l, 512)
    th = _tile(hdim, 512)
    final = final_w is not None
    vec = pl.BlockSpec((None, 1, d), lambda i, j, k: (i, 0, 0))
    in_specs = [
        pl.BlockSpec((None, t, d), lambda i, j, k: (i, j, 0)),
        pl.BlockSpec((1, d), lambda i, j, k: (0, 0)),
        vec, vec, vec,
        pl.BlockSpec((None, d, th), lambda i, j, k: (layer, 0, k)),
        pl.BlockSpec((None, d, th), lambda i, j, k: (layer, 0, k)),
        pl.BlockSpec((None, th, d), lambda i, j, k: (layer, k, 0)),
    ]
    args = [x, nw, shift, scale, gate, wg, wu, wd]
    if final:
        in_specs.append(pl.BlockSpec((1, d), lambda i, j, k: (0, 0)))
        args.append(final_w)
    return pl.pallas_call(
        functools.partial(_ffn_kernel, final=final),
        grid=(b, l // t, hdim // th),
        in_specs=in_specs,
        out_specs=pl.BlockSpec((None, t, d), lambda i, j, k: (i, j, 0)),
        out_shape=jax.ShapeDtypeStruct(x.shape, F32),
        scratch_shapes=[pltpu.VMEM((t, d), BF16)],
        compiler_params=_cparams(("parallel", "parallel", "arbitrary")),
        name="ffn",
    )(*args)


def _chan_dft_kernel(x_ref, nw_ref, sh_ref, sc_ref, w_ref, o_ref, *, gw):
    h = _ada_norm(x_ref[...], nw_ref[...], sh_ref[...], sc_ref[...]).astype(BF16)
    for gi in range(h.shape[1] // gw):
        sl = slice(gi * gw, (gi + 1) * gw)
        r = _dot(h[:, sl], w_ref[...])
        o_ref[0, :, sl] = r[:, :gw].astype(BF16)
        o_ref[1, :, sl] = r[:, gw:].astype(BF16)


def _chan_dft(x, nw, shift, scale, w_cs):
    b, l, d = x.shape
    gw = w_cs.shape[0]
    t = _tile(l, 512)
    vec = pl.BlockSpec((None, 1, d), lambda i, j: (i, 0, 0))
    return pl.pallas_call(
        functools.partial(_chan_dft_kernel, gw=gw),
        grid=(b, l // t),
        in_specs=[
            pl.BlockSpec((None, t, d), lambda i, j: (i, j, 0)),
            pl.BlockSpec((1, d), lambda i, j: (0, 0)),
            vec, vec,
            pl.BlockSpec((gw, 2 * gw), lambda i, j: (0, 0)),
        ],
        out_specs=pl.BlockSpec((None, 2, t, d), lambda i, j: (i, 0, j, 0)),
        out_shape=jax.ShapeDtypeStruct((b, 2, l, d), BF16),
        compiler_params=_cparams(("parallel", "parallel")),
        name="chan_dft",
    )(x, nw, shift, scale, w_cs)


def _seq_dft_kernel(a_ref, pq_ref, o_ref, acc_ref):
    k = pl.program_id(3)

    @pl.when(k == 0)
    def _():
        acc_ref[...] = jnp.zeros_like(acc_ref)

    acc_ref[...] += _dot(a_ref[...], pq_ref[...])

    @pl.when(k == pl.num_programs(3) - 1)
    def _():
        o_ref[...] = acc_ref[...].astype(o_ref.dtype)


def _seq_dft(a_mat, pq):
    b, k2, d = pq.shape
    l = a_mat.shape[0]
    tm = _tile(l, 1024)
    tn = _tile(d, 1024)
    tk = _tile(k2, 2048)
    return pl.pallas_call(
        _seq_dft_kernel,
        grid=(b, l // tm, d // tn, k2 // tk),
        in_specs=[
            pl.BlockSpec((tm, tk), lambda i, m, n, k: (m, k)),
            pl.BlockSpec((None, tk, tn), lambda i, m, n, k: (i, k, n)),
        ],
        out_specs=pl.BlockSpec((None, tm, tn), lambda i, m, n, k: (i, m, n)),
        out_shape=jax.ShapeDtypeStruct((b, l, d), BF16),
        scratch_shapes=[pltpu.VMEM((tm, tn), F32)],
        compiler_params=_cparams(("parallel", "parallel", "parallel", "arbitrary")),
        name="seq_dft",
    )(a_mat, pq)


def _dft_tables(n, scale):
    idx = jnp.arange(n, dtype=jnp.int32)

    def cs(prod):
        ang = (prod % n).astype(F32) * (2.0 * math.pi / n)
        return jnp.cos(ang), jnp.sin(ang)

    m = GRID_WIDTH
    if n <= m or n % m:
        c, s = cs(idx[:, None] * idx[None, :])
        return c * scale, s * scale
    ca, sa = cs(idx[:, None] * (m * jnp.arange(n // m, dtype=jnp.int32))[None, :])
    cb, sb = cs(idx[:, None] * jnp.arange(m, dtype=jnp.int32)[None, :])
    ca, sa = ca[:, :, None] * scale, sa[:, :, None] * scale
    cb, sb = cb[:, None, :], sb[:, None, :]
    return (ca * cb - sa * sb).reshape(n, n), (sa * cb + ca * sb).reshape(n, n)


def _pool_kernel(cur_ref, prev_ref, next_ref, nw_ref, sh_ref, sc_ref, g_ref, w_ref, ps_ref,
                 o_ref, ext_ref, *, t, n_tiles, seq_len, gw):
    l = pl.program_id(1)
    nw, sh, sc = nw_ref[...], sh_ref[...], sc_ref[...]
    hc = _ada_norm(cur_ref[...], nw, sh, sc)
    ext_ref[0:HALO_ROWS, :] = jnp.where(l > 0, _ada_norm(prev_ref[...], nw, sh, sc), 0.0)
    ext_ref[HALO_ROWS:HALO_ROWS + t, :] = hc
    ext_ref[HALO_ROWS + t:, :] = jnp.where(l < n_tiles - 1, _ada_norm(next_ref[...], nw, sh, sc), 0.0)
    pos = l * t + lax.broadcasted_iota(jnp.int32, (t, 1), 0)
    for gi, win in enumerate(POOL_WINDOWS):
        sl = slice(gi * gw, (gi + 1) * gw)
        half = win // 2
        tot = ext_ref[pl.ds(HALO_ROWS - half, t), sl]
        for k in range(-half + 1, win - half):
            tot = tot + ext_ref[pl.ds(HALO_ROWS + k, t), sl]
        cnt = (jnp.minimum(pos + (win - half), seq_len) - jnp.maximum(pos - half, 0)).astype(F32)
        pooled = (tot / cnt - hc[:, sl]).astype(BF16)
        out = _dot(pooled, w_ref[gi]) * ps_ref[:, sl]
        o_ref[:, sl] = cur_ref[:, sl] + g_ref[:, sl] * out


def _pool_mixer(x, nw, shift, scale, gate, w_grp, pscale):
    b, l, d = x.shape
    ng, gw, _ = w_grp.shape
    assert max(POOL_WINDOWS) // 2 <= HALO_ROWS
    t = _tile(l, 512)
    n_tiles = l // t
    rb = t // HALO_ROWS
    last_rb = l // HALO_ROWS - 1
    vec = pl.BlockSpec((None, 1, d), lambda i, j: (i, 0, 0))
    return pl.pallas_call(
        functools.partial(_pool_kernel, t=t, n_tiles=n_tiles, seq_len=l, gw=gw),
        grid=(b, n_tiles),
        in_specs=[
            pl.BlockSpec((None, t, d), lambda i, j: (i, j, 0)),
            pl.BlockSpec((None, HALO_ROWS, d), lambda i, j: (i, jnp.maximum(j * rb - 1, 0), 0)),
            pl.BlockSpec((None, HALO_ROWS, d), lambda i, j: (i, jnp.minimum((j + 1) * rb, last_rb), 0)),
            pl.BlockSpec((1, d), lambda i, j: (0, 0)),
            vec, vec, vec,
            pl.BlockSpec((ng, gw, gw), lambda i, j: (0, 0, 0)),
            pl.BlockSpec((1, d), lambda i, j: (0, 0)),
        ],
        out_specs=pl.BlockSpec((None, t, d), lambda i, j: (i, j, 0)),
        out_shape=jax.ShapeDtypeStruct(x.shape, F32),
        scratch_shapes=[pltpu.VMEM((t + 2 * HALO_ROWS, d), F32)],
        compiler_params=_cparams(("parallel", "parallel")),
        name="pool_mixer",
    )(x, x, x, nw, shift, scale, gate, w_grp, pscale)


def _pos_tables(n_tokens, d):
    rows = n_tokens // GRID_WIDTH
    quarter = d // 4
    omega = 1.0 / (POS_BASE ** (jnp.arange(quarter, dtype=F32) / quarter))

    def table(n):
        v = jnp.arange(n, dtype=F32)[:, None] * omega
        return jnp.concatenate([jnp.sin(v), jnp.cos(v)], axis=-1)

    return table(rows), table(GRID_WIDTH)


def _flat(u):
    return u.reshape((1, u.shape[0] * u.shape[1]) + u.shape[2:])


def _unflat(u, b):
    return u.reshape((b, u.shape[0] * u.shape[1] // b) + u.shape[2:])


def _ssd_layer(x, ctx, mods, mods_c, nw, p, with_ctx, pos=None):
    sh1, sc1, g1 = mods[0], mods[1], mods[2]
    ch1, cs1, cg1 = mods_c[0], mods_c[1], mods_c[2]
    d_inner = p["out_proj"].shape[1]
    heads = p["dt_bias"].shape[-1]
    hpg = heads // SSD_GROUPS
    gw = d_inner // SSD_GROUPS
    dt_bias = p["dt_bias"].reshape(1, 2 * heads)
    alog = p["a_log"].reshape(2 * SSD_GROUPS, hpg)
    skip_diag = (p["d"].reshape(2 * SSD_GROUPS, hpg)[:, :, None, None]
               * jnp.eye(SSD_CHUNK, dtype=F32)[None, None]).astype(BF16)

    b = x.shape[0]

    def project(u, shift, scale, u_pos=None):
        zx, dts = _in_proj(u, u_pos, nw, shift, scale, p["in_proj"], p["layer"], dt_bias)
        xbc = _conv_silu(_unflat(zx, b), p["conv_w"], p["conv_b"], d_inner)
        dtc5, dtr5 = _dt_layouts(_unflat(dts, b), hpg)
        return zx, xbc, dtc5, dtr5

    zx_c, xbc_c, dtc_c, dtr_c = project(_flat(ctx), ch1, cs1)
    zx_l, xbc_l, dtc_l, dtr_l = project(x, sh1, sc1, pos)
    h0 = jnp.zeros((b, 2, SSD_GROUPS, SSD_STATE, gw), F32)
    y_c, st_c = _ssd_scan(xbc_c, dtc_c, dtr_c, alog, skip_diag, h0, d_inner)
    y_l, _ = _ssd_scan(xbc_l, dtc_l, dtr_l, alog, skip_diag, st_c, d_inner)
    x = _ssd_out(y_l, zx_l, p["norm_w"], p["out_proj"], p["layer"], x, g1, pos)
    if with_ctx:
        y_flat = y_c.reshape(2, 1, -1, d_inner)
        ctx = _unflat(_ssd_out(y_flat, zx_c, p["norm_w"], p["out_proj"], p["layer"], _flat(ctx), cg1), b)
    return x, ctx


def _fourier_layer(x, ctx, mods, mods_c, nw, w_out, with_ctx):
    d = x.shape[-1]
    gw = d // N_FOURIER_GROUPS
    cc, sc_ = _dft_tables(gw, gw ** -0.5)
    w_cs = jnp.concatenate([cc, sc_], axis=1).astype(BF16)

    def mix(u, shift, scale, gate):
        l = u.shape[1]
        cl, sl = _dft_tables(l, l ** -0.5)
        a_mat = jnp.concatenate([cl, -sl], axis=1).astype(BF16)
        pq = _chan_dft(u, nw, shift, scale, w_cs)
        f = _seq_dft(a_mat, pq.reshape(u.shape[0], 2 * l, d))
        if gate.shape[0] == 1:
            return _unflat(_matmul_residual(_flat(f), w_out, _flat(u), gate), u.shape[0])
        return _matmul_residual(f, w_out, u, gate)

    x = mix(x, mods[0], mods[1], mods[2])
    if with_ctx:
        b = ctx.shape[0]
        ctx = mix(ctx, jnp.broadcast_to(mods_c[0], (b, 1, d)), jnp.broadcast_to(mods_c[1], (b, 1, d)), mods_c[2])
    return x, ctx


def kernel(x, c, ctx, c_ctx, w_mod, b_mod, norm_w, ffn_w_gate, ffn_w_up, ffn_w_down, ssd_in_proj, ssd_conv_w,
           ssd_conv_b, ssd_dt_bias, ssd_a_log, ssd_d, ssd_norm_w, ssd_out_proj, fourier_w_out, pool_w, pool_scale,
           final_norm_w):
    b, l, d = x.shape
    depth = w_mod.shape[0]
    n_ctx_rows = V7X_SUBLANES
    c_all = jnp.concatenate([c, jnp.broadcast_to(c_ctx[None, :], (n_ctx_rows, d))], axis=0)
    mod_all = _modulation(c_all, w_mod, b_mod)

    wg16, wu16, wd16 = ffn_w_gate.astype(BF16), ffn_w_up.astype(BF16), ffn_w_down.astype(BF16)
    in_proj16, out_proj16 = ssd_in_proj.astype(BF16), ssd_out_proj.astype(BF16)

    pos = _pos_tables(l, d)
    for i in range(depth):
        kind, j = i % N_MIXERS, i // N_MIXERS
        last = i == depth - 1
        mods = [mod_all[i, :b, None, k * d:(k + 1) * d] for k in range(6)]
        mods_c = [mod_all[i, b:b + 1, None, k * d:(k + 1) * d] for k in range(6)]
        nw1 = norm_w[i, 0][None, :]
        nw2 = norm_w[i, 1][None, :]
        if kind == 0:
            p = dict(in_proj=in_proj16, out_proj=out_proj16, layer=j, conv_w=ssd_conv_w[j],
                     conv_b=ssd_conv_b[j][None, :], dt_bias=ssd_dt_bias[j], a_log=ssd_a_log[j], d=ssd_d[j],
                     norm_w=ssd_norm_w[j][None, :])
            x, ctx = _ssd_layer(x, ctx, mods, mods_c, nw1, p, not last, pos if i == 0 else None)
        elif kind == 1:
            x, ctx = _fourier_layer(x, ctx, mods, mods_c, nw1, fourier_w_out[j].astype(BF16), not last)
        else:
            pw = pool_w[j].astype(BF16)
            ps = pool_scale[j][None, :]
            x = _pool_mixer(x, nw1, mods[0], mods[1], mods[2], pw, ps)
            if not last:
                ctx = _pool_mixer(ctx, nw1, *[jnp.broadcast_to(m, (b, 1, d)) for m in mods_c[:3]], pw, ps)
        fw = final_norm_w[None, :] if last else None
        x = _ffn(x, nw2, mods[3], mods[4], mods[5], wg16, wu16, wd16, i, fw)
        if not last:
            ctx = _unflat(_ffn(_flat(ctx), nw2, mods_c[3], mods_c[4], mods_c[5], wg16, wu16, wd16, i), b)
    return x
```

```python
import functools
import math

import jax
import jax.numpy as jnp
from jax import lax
from jax.experimental import pallas as pl
from jax.experimental.pallas import tpu as pltpu

F32 = jnp.float32
BF16 = jnp.bfloat16

GRID_WIDTH = 64
POS_BASE = 10000.0
NORM_EPS = 1e-6
SSD_NORM_EPS = 1e-5
SSD_HEAD_DIM = 64
SSD_GROUPS = 8
SSD_STATE = 128
SSD_CHUNK = 128
N_FOURIER_GROUPS = 4
POOL_WINDOWS = (2, 4, 8, 16)
N_MIXERS = 3
SCAN_CHUNKS_PER_STEP = 16

V7X_LANES = 128
V7X_SUBLANES = 8
V7X_VMEM_BYTES = 64 * 1024 * 1024
V7X_VMEM_LIMIT_BYTES = V7X_VMEM_BYTES - 8 * 1024 * 1024
HALO_ROWS = V7X_SUBLANES
NEG_BIG = -1e30
LOG2_E = 1.4426950408889634


def _cparams(semantics):
    return pltpu.CompilerParams(dimension_semantics=semantics,
                                vmem_limit_bytes=V7X_VMEM_LIMIT_BYTES)


def _tile(n, pref):
    t = min(n, pref)
    assert n % t == 0, (n, t)
    return t


def _silu(v):
    h = 0.5 * v
    return h + h * jnp.tanh(h)


def _ada_norm(x, nw, shift, scale):
    ms = jnp.mean(x * x, axis=-1, keepdims=True)
    y = x * lax.rsqrt(ms + NORM_EPS)
    return (y * nw) * (1.0 + scale) + shift


def _pos_tile(rt_ref, ct_ref, tile_idx, t):
    w = GRID_WIDTH
    n_rows = t // w
    half = rt_ref.shape[1]
    r0 = tile_idx * n_rows
    left = jnp.concatenate([jnp.broadcast_to(rt_ref[pl.ds(r0 + rr, 1), :], (w, half)) for rr in range(n_rows)],
                           axis=0)
    right = jnp.concatenate([ct_ref[...]] * n_rows, axis=0)
    return jnp.concatenate([left, right], axis=1)


def _split3(v):
    hi = v.astype(BF16)
    r1 = v - hi.astype(F32)
    mid = r1.astype(BF16)
    lo = (r1 - mid.astype(F32)).astype(BF16)
    return hi, mid, lo


def _dot(a, b):
    return jnp.dot(a, b, preferred_element_type=F32)


def _mod_kernel(c_ref, w_ref, b_ref, o_ref):
    s = _silu(c_ref[...])
    o_ref[...] = jnp.dot(s, w_ref[...], preferred_element_type=F32,
                         precision=lax.Precision.HIGHEST) + b_ref[...]


def _modulation(c_all, w_mod, b_mod):
    depth, d, n = w_mod.shape
    rows = c_all.shape[0]
    tn = _tile(n, 1024)
    return pl.pallas_call(
        _mod_kernel,
        grid=(depth, n // tn),
        in_specs=[
            pl.BlockSpec((rows, d), lambda i, j: (0, 0)),
            pl.BlockSpec((None, d, tn), lambda i, j: (i, 0, j)),
            pl.BlockSpec((None, 1, tn), lambda i, j: (i, 0, j)),
        ],
        out_specs=pl.BlockSpec((None, rows, tn), lambda i, j: (i, 0, j)),
        out_shape=jax.ShapeDtypeStruct((depth, rows, n), F32),
        compiler_params=_cparams(("parallel", "parallel")),
        name="modulation",
    )(c_all, w_mod, b_mod.reshape(depth, 1, n))


def _in_proj_kernel(x_ref, *rest, with_pos):
    if with_pos:
        rt_ref, ct_ref, nw_ref, sh_ref, sc_ref, w_ref, wdt_ref, b_ref, o_ref, dt_ref, h_ref = rest
    else:
        nw_ref, sh_ref, sc_ref, w_ref, wdt_ref, b_ref, o_ref, dt_ref, h_ref = rest

    @pl.when(pl.program_id(2) == 0)
    def _():
        xv = x_ref[...]
        if with_pos:
            xv = xv + _pos_tile(rt_ref, ct_ref, pl.program_id(1), xv.shape[0])
        h = _ada_norm(xv, nw_ref[...], sh_ref[...], sc_ref[...]).astype(BF16)
        h_ref[...] = h
        v = _dot(h, wdt_ref[...]) + b_ref[...]
        dt_ref[...] = jnp.maximum(v, 0.0) + jnp.log1p(jnp.exp(-jnp.abs(v)))

    o_ref[...] = _dot(h_ref[...], w_ref[...]).astype(o_ref.dtype)


def _in_proj(x, pos, nw, shift, scale, w, layer, dt_bias):
    b, l, d = x.shape
    ndt = dt_bias.shape[1]
    n = w.shape[2] - ndt
    assert n % ndt == 0
    dt_block = n // ndt
    t = _tile(l, 1024)
    with_pos = pos is not None
    tn = _tile(n, 1024)
    pos_args = list(pos) if with_pos else []
    pos_specs = [pl.BlockSpec(a.shape, lambda i, j, k: (0, 0)) for a in pos_args]
    assert not with_pos or t % GRID_WIDTH == 0
    return pl.pallas_call(
        functools.partial(_in_proj_kernel, with_pos=with_pos),
        grid=(b, l // t, n // tn),
        in_specs=[pl.BlockSpec((None, t, d), lambda i, j, k: (i, j, 0))] + pos_specs + [
            pl.BlockSpec((1, d), lambda i, j, k: (0, 0)),
            pl.BlockSpec((None, 1, d), lambda i, j, k: (i, 0, 0)),
            pl.BlockSpec((None, 1, d), lambda i, j, k: (i, 0, 0)),
            pl.BlockSpec((None, d, tn), lambda i, j, k: (layer, 0, k)),
            pl.BlockSpec((None, d, ndt), lambda i, j, k: (layer, 0, dt_block)),
            pl.BlockSpec((1, ndt), lambda i, j, k: (0, 0)),
        ],
        out_specs=[pl.BlockSpec((None, t, tn), lambda i, j, k: (i, j, k)),
                   pl.BlockSpec((None, t, ndt), lambda i, j, k: (i, j, 0))],
        out_shape=[jax.ShapeDtypeStruct((b, l, n), BF16), jax.ShapeDtypeStruct((b, l, ndt), F32)],
        scratch_shapes=[pltpu.VMEM((t, d), BF16)],
        compiler_params=_cparams(("parallel", "parallel", "arbitrary")),
        name="in_proj",
    )(x, *pos_args, nw, shift, scale, w, w, dt_bias)


def _conv_kernel(cur_ref, prev_ref, next_ref, sel_ref, w_ref, b_ref, o_ref, *, t, n_tiles, width):
    l = pl.program_id(1)
    pad = width // 2
    halo = prev_ref.shape[0]
    sb = sel_ref.shape[1]
    zero = jnp.zeros(prev_ref.shape, prev_ref.dtype)
    prev = jnp.where(l > 0, prev_ref[...], zero)
    nxt = jnp.where(l < n_tiles - 1, next_ref[...], zero)
    ext = jnp.concatenate([prev, cur_ref[...], nxt], axis=0)
    for s in range(t // sb):
        win = ext[s * sb:(s + 1) * sb + 2 * halo]
        acc = b_ref[...] + w_ref[pad:pad + 1, :] * win[halo:halo + sb].astype(F32)
        for k in range(width):
            if k != pad:
                acc = acc + w_ref[k:k + 1, :] * _dot(sel_ref[k], win)
        o_ref[s * sb:(s + 1) * sb, :] = _silu(acc).astype(o_ref.dtype)


def _conv_silu(zx, conv_w, conv_b, col_offset):
    b, l, _ = zx.shape
    width, c = conv_w.shape
    pad = width // 2
    halo = 2 * V7X_SUBLANES
    assert pad <= halo
    t = _tile(l, 512)
    tc = _tile(c, 2048)
    sb = _tile(t, V7X_LANES)
    n_tiles = l // t
    off = col_offset // tc
    hb = t // halo
    last_hb = l // halo - 1
    i_idx = jnp.arange(sb)[None, :, None]
    j_idx = jnp.arange(sb + 2 * halo)[None, None, :]
    k_idx = jnp.arange(width)[:, None, None]
    sel = (j_idx == i_idx + halo + k_idx - pad).astype(BF16)
    return pl.pallas_call(
        functools.partial(_conv_kernel, t=t, n_tiles=n_tiles, width=width),
        grid=(b, n_tiles, c // tc),
        in_specs=[
            pl.BlockSpec((None, t, tc), lambda i, j, k: (i, j, off + k)),
            pl.BlockSpec((None, halo, tc), lambda i, j, k: (i, jnp.maximum(j * hb - 1, 0), off + k)),
            pl.BlockSpec((None, halo, tc), lambda i, j, k: (i, jnp.minimum((j + 1) * hb, last_hb), off + k)),
            pl.BlockSpec((width, sb, sb + 2 * halo), lambda i, j, k: (0, 0, 0)),
            pl.BlockSpec((width, tc), lambda i, j, k: (0, k)),
            pl.BlockSpec((1, tc), lambda i, j, k: (0, k)),
        ],
        out_specs=pl.BlockSpec((None, t, tc), lambda i, j, k: (i, j, k)),
        out_shape=jax.ShapeDtypeStruct((b, l, c), BF16),
        compiler_params=_cparams(("parallel", "parallel", "parallel")),
        name="conv_silu",
    )(zx, zx, zx, sel, conv_w, conv_b)


def _scan_kernel(xs_ref, b_ref, c_ref, dtc_ref, dtr_ref, alr_ref, alc_ref, skd_ref, h0_ref,
                 tm_ref, tmt_ref, madd_ref, y_ref, st_ref, *, cps, heads):
    d = pl.program_id(1)
    s = pl.program_id(3)
    q = SSD_CHUNK
    p = SSD_HEAD_DIM

    @pl.when(s == 0)
    def _():
        st_ref[...] = h0_ref[...]

    lane = lax.broadcasted_iota(jnp.int32, (q, 2 * p), 1)
    lo_half = lane < p
    tm = tm_ref[...]
    tm_t = tmt_ref[...]

    dtr = dtr_ref[...]
    la_c = dtc_ref[...] * (-jnp.exp(alr_ref[...]) * LOG2_E)
    la_r = dtr * (-jnp.exp(alc_ref[...]) * LOG2_E)
    c_hi, c_mid, c_lo = _split3(la_c)
    cum_c = _dot(tm, c_hi) + _dot(tm, c_mid) + _dot(tm, c_lo)
    r_hi, r_mid, r_lo = _split3(la_r)
    cum_r = _dot(r_hi, tm_t) + _dot(r_mid, tm_t) + _dot(r_lo, tm_t)
    tot_r = jnp.sum(la_r, axis=1, keepdims=True)
    ldt_r = jnp.log2(dtr)
    f_r = jnp.exp2(tot_r - cum_r + ldt_r)
    rowb_all = cum_r - ldt_r

    st = st_ref[...]
    for i in range(cps):
        j = i + d * (cps - 1 - 2 * i)
        off = pl.multiple_of(j * q, q)
        x16 = xs_ref[pl.ds(off, q), :]
        b16 = b_ref[pl.ds(off, q), :]
        c16 = c_ref[pl.ds(off, q), :]
        bt16 = b16.astype(F32).T.astype(BF16)
        scores16 = _dot(c16, bt16).astype(BF16)
        y_state = _dot(c16, st.astype(BF16))

        y_parts, upd_parts, ec_parts = [], [], []
        for pair in range(heads // 2):
            ms, bs, ecs = [], [], []
            for e in (2 * pair, 2 * pair + 1):
                k = i * heads + e
                colb = jnp.broadcast_to(cum_c[:, k:k + 1], (q, q))
                dec = jnp.exp2((colb - rowb_all[k:k + 1, :]) + madd_ref[...])
                ms.append(scores16 * dec.astype(BF16) + skd_ref[e])
                bs.append(bt16 * jnp.broadcast_to(f_r[k:k + 1, :], (q, q)).astype(BF16))
                ecs.append(jnp.exp2(colb))
            xp = x16[:, pair * 2 * p:(pair + 1) * 2 * p]
            zero = jnp.zeros_like(xp)
            rhs = jnp.concatenate([jnp.where(lo_half, xp, zero), jnp.where(lo_half, zero, xp)], axis=0)
            y_parts.append(_dot(jnp.concatenate(ms, axis=1), rhs))
            upd_parts.append(_dot(jnp.concatenate(bs, axis=1), rhs))
            ec_parts.append(jnp.where(lo_half, ecs[0], ecs[1]))
        ec = jnp.concatenate(ec_parts, axis=1)
        y = jnp.concatenate(y_parts, axis=1) + y_state * ec
        y_ref[pl.ds(off, q), :] = y.astype(y_ref.dtype)
        etot = jnp.where(d == 0, ec[q - 1:q, :], ec[0:1, :])
        st = st * etot + jnp.concatenate(upd_parts, axis=1)
    st_ref[...] = st


def _scan_masks():
    q = SSD_CHUNK
    row = jnp.arange(q)[:, None]
    col = jnp.arange(q)[None, :]
    tri = jnp.stack([row >= col, row <= col])
    return (tri.astype(BF16), jnp.swapaxes(tri, 1, 2).astype(BF16),
            jnp.where(tri, 0.0, NEG_BIG).astype(F32))


def _scan_chunks_per_step(l):
    return min(l // SSD_CHUNK, SCAN_CHUNKS_PER_STEP)


def _dt_layouts(dts, heads):
    b, l, _ = dts.shape
    cps = _scan_chunks_per_step(l)
    steps = l // (cps * SSD_CHUNK)
    v = dts.reshape(b, steps, cps, SSD_CHUNK, 2, SSD_GROUPS, heads)
    v = jnp.stack([v[:, :, :, :, 0], v[:, :, ::-1, :, 1]], axis=1)
    col = v.transpose(0, 1, 5, 2, 4, 3, 6).reshape(b, 2 * SSD_GROUPS, steps, SSD_CHUNK, cps * heads)
    row = v.transpose(0, 1, 5, 2, 3, 6, 4).reshape(b, 2 * SSD_GROUPS, steps, cps * heads, SSD_CHUNK)
    return col, row


def _ssd_scan(xbc, dtc5, dtr5, alog, skip_diag, h0, d_inner):
    b, l, _ = xbc.shape
    g = SSD_GROUPS
    n = SSD_STATE
    gw = d_inner // g
    heads = gw // SSD_HEAD_DIM
    cps = _scan_chunks_per_step(l)
    r = cps * SSD_CHUNK
    steps = l // r
    b_off = d_inner // n
    c_off = (d_inner + g * n) // n
    alog_t = jnp.tile(alog, (1, cps))
    alog_row = alog_t[:, None, :]
    alog_col = alog_t[:, :, None]

    def cb(si, di):
        return si + di * (steps - 1 - 2 * si)

    tm, tm_t, madd = _scan_masks()
    mask_spec = pl.BlockSpec((None, SSD_CHUNK, SSD_CHUNK), lambda bi, di, gi, si: (di, 0, 0))
    return pl.pallas_call(
        functools.partial(_scan_kernel, cps=cps, heads=heads),
        grid=(b, 2, g, steps),
        in_specs=[
            pl.BlockSpec((None, r, gw), lambda bi, di, gi, si: (bi, cb(si, di), gi)),
            pl.BlockSpec((None, r, n), lambda bi, di, gi, si: (bi, cb(si, di), b_off + gi)),
            pl.BlockSpec((None, r, n), lambda bi, di, gi, si: (bi, cb(si, di), c_off + gi)),
            pl.BlockSpec((None, None, None, SSD_CHUNK, cps * heads),
                         lambda bi, di, gi, si: (bi, di * g + gi, cb(si, di), 0, 0)),
            pl.BlockSpec((None, None, None, cps * heads, SSD_CHUNK),
                         lambda bi, di, gi, si: (bi, di * g + gi, cb(si, di), 0, 0)),
            pl.BlockSpec((None, 1, cps * heads), lambda bi, di, gi, si: (di * g + gi, 0, 0)),
            pl.BlockSpec((None, cps * heads, 1), lambda bi, di, gi, si: (di * g + gi, 0, 0)),
            pl.BlockSpec((None, heads, SSD_CHUNK, SSD_CHUNK), lambda bi, di, gi, si: (di * g + gi, 0, 0, 0)),
            pl.BlockSpec((None, None, None, n, gw), lambda bi, di, gi, si: (bi, di, gi, 0, 0)),
            mask_spec, mask_spec, mask_spec,
        ],
        out_specs=[
            pl.BlockSpec((None, None, r, gw), lambda bi, di, gi, si: (di, bi, cb(si, di), gi)),
            pl.BlockSpec((None, None, None, n, gw), lambda bi, di, gi, si: (bi, di, gi, 0, 0)),
        ],
        out_shape=[jax.ShapeDtypeStruct((2, b, l, d_inner), BF16),
                   jax.ShapeDtypeStruct((b, 2, g, n, gw), F32)],
        compiler_params=_cparams(("parallel", "parallel", "parallel", "arbitrary")),
        name="ssd_scan",
    )(xbc, xbc, xbc, dtc5, dtr5, alog_row, alog_col, skip_diag, h0, tm, tm_t, madd)


def _ssd_out_kernel(yf_ref, yb_ref, z_ref, nw_ref, w_ref, x_ref, g_ref, *rest, gw, with_pos):
    if with_pos:
        rt_ref, ct_ref, o_ref, a_ref = rest
    else:
        o_ref, a_ref = rest
    for gi in range(a_ref.shape[1] // gw):
        sl = slice(gi * gw, (gi + 1) * gw)
        v = (yf_ref[:, sl].astype(F32) + yb_ref[:, sl].astype(F32)) * _silu(z_ref[:, sl].astype(F32))
        ms = jnp.mean(v * v, axis=-1, keepdims=True)
        a_ref[:, sl] = (v * lax.rsqrt(ms + SSD_NORM_EPS) * nw_ref[:, sl]).astype(BF16)
    xv = x_ref[...]
    if with_pos:
        xv = xv + _pos_tile(rt_ref, ct_ref, pl.program_id(1), xv.shape[0])
    o_ref[...] = xv + g_ref[...] * _dot(a_ref[...], w_ref[...])


def _ssd_out(y2, zx, norm_w, w_out, layer, x, gate, pos=None):
    b, l, d = x.shape
    k = w_out.shape[1]
    t = _tile(l, 256)
    with_pos = pos is not None
    pos_args = list(pos) if with_pos else []
    pos_specs = [pl.BlockSpec(a.shape, lambda i, j: (0, 0)) for a in pos_args]
    assert not with_pos or t % GRID_WIDTH == 0
    return pl.pallas_call(
        functools.partial(_ssd_out_kernel, gw=k // SSD_GROUPS, with_pos=with_pos),
        grid=(b, l // t),
        in_specs=[
            pl.BlockSpec((None, None, t, k), lambda i, j: (0, i, j, 0)),
            pl.BlockSpec((None, None, t, k), lambda i, j: (1, i, j, 0)),
            pl.BlockSpec((None, t, k), lambda i, j: (i, j, 0)),
            pl.BlockSpec((1, k), lambda i, j: (0, 0)),
            pl.BlockSpec((None, k, d), lambda i, j: (layer, 0, 0), pipeline_mode=pl.Buffered(1)),
            pl.BlockSpec((None, t, d), lambda i, j: (i, j, 0)),
            pl.BlockSpec((None, 1, d), lambda i, j: (i, 0, 0)),
        ] + pos_specs,
        out_specs=pl.BlockSpec((None, t, d), lambda i, j: (i, j, 0)),
        out_shape=jax.ShapeDtypeStruct(x.shape, F32),
        scratch_shapes=[pltpu.VMEM((t, k), BF16)],
        compiler_params=_cparams(("parallel", "parallel")),
        name="ssd_out",
    )(y2, y2, zx, norm_w, w_out, x, gate, *pos_args)


def _mm_res_kernel(a_ref, w_ref, x_ref, g_ref, o_ref):
    o_ref[...] = x_ref[...] + g_ref[...] * _dot(a_ref[...], w_ref[...])


def _matmul_residual(a, w, x, gate):
    b, l, d = x.shape
    k = w.shape[0]
    t = _tile(l, 1024)
    tn = _tile(d, 1024)
    return pl.pallas_call(
        _mm_res_kernel,
        grid=(b, l // t, d // tn),
        in_specs=[
            pl.BlockSpec((None, t, k), lambda i, j, n: (i, j, 0)),
            pl.BlockSpec((k, tn), lambda i, j, n: (0, n)),
            pl.BlockSpec((None, t, tn), lambda i, j, n: (i, j, n)),
            pl.BlockSpec((None, 1, tn), lambda i, j, n: (i, 0, n)),
        ],
        out_specs=pl.BlockSpec((None, t, tn), lambda i, j, n: (i, j, n)),
        out_shape=jax.ShapeDtypeStruct(x.shape, F32),
        compiler_params=_cparams(("parallel", "parallel", "parallel")),
        name="matmul_residual",
    )(a, w, x, gate)


def _ffn_kernel(x_ref, nw_ref, sh_ref, sc_ref, g_ref, wg_ref, wu_ref, wd_ref, *rest, final):
    if final:
        fw_ref, o_ref, h_ref = rest
    else:
        o_ref, h_ref = rest
    j = pl.program_id(2)

    @pl.when(j == 0)
    def _():
        h_ref[...] = _ada_norm(x_ref[...], nw_ref[...], sh_ref[...], sc_ref[...]).astype(BF16)
        o_ref[...] = jnp.zeros_like(o_ref)

    h = h_ref[...]
    a = _dot(h, wg_ref[...])
    u = _dot(h, wu_ref[...])
    o_ref[...] += _dot((_silu(a) * u).astype(BF16), wd_ref[...])

    @pl.when(j == pl.num_programs(2) - 1)
    def _():
        out = x_ref[...] + g_ref[...] * o_ref[...]
        if final:
            ms = jnp.mean(out * out, axis=-1, keepdims=True)
            out = out * lax.rsqrt(ms + NORM_EPS) * fw_ref[...]
        o_ref[...] = out


def _ffn(x, nw, shift, scale, gate, wg, wu, wd, layer, final_w=None):
    b, l, d = x.shape
    hdim = wg.shape[2]
    t = _tile(l, 512)
    th = _tile(hdim, 512)
    final = final_w is not None
    vec = pl.BlockSpec((None, 1, d), lambda i, j, k: (i, 0, 0))
    in_specs = [
        pl.BlockSpec((None, t, d), lambda i, j, k: (i, j, 0)),
        pl.BlockSpec((1, d), lambda i, j, k: (0, 0)),
        vec, vec, vec,
        pl.BlockSpec((None, d, th), lambda i, j, k: (layer, 0, k)),
        pl.BlockSpec((None, d, th), lambda i, j, k: (layer, 0, k)),
        pl.BlockSpec((None, th, d), lambda i, j, k: (layer, k, 0)),
    ]
    args = [x, nw, shift, scale, gate, wg, wu, wd]
    if final:
        in_specs.append(pl.BlockSpec((1, d), lambda i, j, k: (0, 0)))
        args.append(final_w)
    return pl.pallas_call(
        functools.partial(_ffn_kernel, final=final),
        grid=(b, l // t, hdim // th),
        in_specs=in_specs,
        out_specs=pl.BlockSpec((None, t, d), lambda i, j, k: (i, j, 0)),
        out_shape=jax.ShapeDtypeStruct(x.shape, F32),
        scratch_shapes=[pltpu.VMEM((t, d), BF16)],
        compiler_params=_cparams(("parallel", "parallel", "arbitrary")),
        name="ffn",
    )(*args)


def _chan_dft_kernel(x_ref, nw_ref, sh_ref, sc_ref, w_ref, o_ref, *, gw):
    h = _ada_norm(x_ref[...], nw_ref[...], sh_ref[...], sc_ref[...]).astype(BF16)
    for gi in range(h.shape[1] // gw):
        sl = slice(gi * gw, (gi + 1) * gw)
        r = _dot(h[:, sl], w_ref[...])
        o_ref[0, :, sl] = r[:, :gw].astype(BF16)
        o_ref[1, :, sl] = r[:, gw:].astype(BF16)


def _chan_dft(x, nw, shift, scale, w_cs):
    b, l, d = x.shape
    gw = w_cs.shape[0]
    t = _tile(l, 512)
    vec = pl.BlockSpec((None, 1, d), lambda i, j: (i, 0, 0))
    return pl.pallas_call(
        functools.partial(_chan_dft_kernel, gw=gw),
        grid=(b, l // t),
        in_specs=[
            pl.BlockSpec((None, t, d), lambda i, j: (i, j, 0)),
            pl.BlockSpec((1, d), lambda i, j: (0, 0)),
            vec, vec,
            pl.BlockSpec((gw, 2 * gw), lambda i, j: (0, 0)),
        ],
        out_specs=pl.BlockSpec((None, 2, t, d), lambda i, j: (i, 0, j, 0)),
        out_shape=jax.ShapeDtypeStruct((b, 2, l, d), BF16),
        compiler_params=_cparams(("parallel", "parallel")),
        name="chan_dft",
    )(x, nw, shift, scale, w_cs)


def _seq_dft_kernel(a_ref, pq_ref, o_ref, acc_ref):
    k = pl.program_id(3)

    @pl.when(k == 0)
    def _():
        acc_ref[...] = jnp.zeros_like(acc_ref)

    acc_ref[...] += _dot(a_ref[...], pq_ref[...])

    @pl.when(k == pl.num_programs(3) - 1)
    def _():
        o_ref[...] = acc_ref[...].astype(o_ref.dtype)


def _seq_dft(a_mat, pq):
    b, k2, d = pq.shape
    l = a_mat.shape[0]
    tm = _tile(l, 1024)
    tn = _tile(d, 1024)
    tk = _tile(k2, 2048)
    return pl.pallas_call(
        _seq_dft_kernel,
        grid=(b, l // tm, d // tn, k2 // tk),
        in_specs=[
            pl.BlockSpec((tm, tk), lambda i, m, n, k: (m, k)),
            pl.BlockSpec((None, tk, tn), lambda i, m, n, k: (i, k, n)),
        ],
        out_specs=pl.BlockSpec((None, tm, tn), lambda i, m, n, k: (i, m, n)),
        out_shape=jax.ShapeDtypeStruct((b, l, d), BF16),
        scratch_shapes=[pltpu.VMEM((tm, tn), F32)],
        compiler_params=_cparams(("parallel", "parallel", "parallel", "arbitrary")),
        name="seq_dft",
    )(a_mat, pq)


def _dft_tables(n, scale):
    idx = jnp.arange(n, dtype=jnp.int32)

    def cs(prod):
        ang = (prod % n).astype(F32) * (2.0 * math.pi / n)
        return jnp.cos(ang), jnp.sin(ang)

    m = GRID_WIDTH
    if n <= m or n % m:
        c, s = cs(idx[:, None] * idx[None, :])
        return c * scale, s * scale
    ca, sa = cs(idx[:, None] * (m * jnp.arange(n // m, dtype=jnp.int32))[None, :])
    cb, sb = cs(idx[:, None] * jnp.arange(m, dtype=jnp.int32)[None, :])
    ca, sa = ca[:, :, None] * scale, sa[:, :, None] * scale
    cb, sb = cb[:, None, :], sb[:, None, :]
    return (ca * cb - sa * sb).reshape(n, n), (sa * cb + ca * sb).reshape(n, n)


def _pool_kernel(cur_ref, prev_ref, next_ref, nw_ref, sh_ref, sc_ref, g_ref, w_ref, ps_ref,
                 o_ref, ext_ref, *, t, n_tiles, seq_len, gw):
    l = pl.program_id(1)
    nw, sh, sc = nw_ref[...], sh_ref[...], sc_ref[...]
    hc = _ada_norm(cur_ref[...], nw, sh, sc)
    ext_ref[0:HALO_ROWS, :] = jnp.where(l > 0, _ada_norm(prev_ref[...], nw, sh, sc), 0.0)
    ext_ref[HALO_ROWS:HALO_ROWS + t, :] = hc
    ext_ref[HALO_ROWS + t:, :] = jnp.where(l < n_tiles - 1, _ada_norm(next_ref[...], nw, sh, sc), 0.0)
    pos = l * t + lax.broadcasted_iota(jnp.int32, (t, 1), 0)
    for gi, win in enumerate(POOL_WINDOWS):
        sl = slice(gi * gw, (gi + 1) * gw)
        half = win // 2
        tot = ext_ref[pl.ds(HALO_ROWS - half, t), sl]
        for k in range(-half + 1, win - half):
            tot = tot + ext_ref[pl.ds(HALO_ROWS + k, t), sl]
        cnt = (jnp.minimum(pos + (win - half), seq_len) - jnp.maximum(pos - half, 0)).astype(F32)
        pooled = (tot / cnt - hc[:, sl]).astype(BF16)
        out = _dot(pooled, w_ref[gi]) * ps_ref[:, sl]
        o_ref[:, sl] = cur_ref[:, sl] + g_ref[:, sl] * out


def _pool_mixer(x, nw, shift, scale, gate, w_grp, pscale):
    b, l, d = x.shape
    ng, gw, _ = w_grp.shape
    assert max(POOL_WINDOWS) // 2 <= HALO_ROWS
    t = _tile(l, 512)
    n_tiles = l // t
    rb = t // HALO_ROWS
    last_rb = l // HALO_ROWS - 1
    vec = pl.BlockSpec((None, 1, d), lambda i, j: (i, 0, 0))
    return pl.pallas_call(
        functools.partial(_pool_kernel, t=t, n_tiles=n_tiles, seq_len=l, gw=gw),
        grid=(b, n_tiles),
        in_specs=[
            pl.BlockSpec((None, t, d), lambda i, j: (i, j, 0)),
            pl.BlockSpec((None, HALO_ROWS, d), lambda i, j: (i, jnp.maximum(j * rb - 1, 0), 0)),
            pl.BlockSpec((None, HALO_ROWS, d), lambda i, j: (i, jnp.minimum((j + 1) * rb, last_rb), 0)),
            pl.BlockSpec((1, d), lambda i, j: (0, 0)),
            vec, vec, vec,
            pl.BlockSpec((ng, gw, gw), lambda i, j: (0, 0, 0)),
            pl.BlockSpec((1, d), lambda i, j: (0, 0)),
        ],
        out_specs=pl.BlockSpec((None, t, d), lambda i, j: (i, j, 0)),
        out_shape=jax.ShapeDtypeStruct(x.shape, F32),
        scratch_shapes=[pltpu.VMEM((t + 2 * HALO_ROWS, d), F32)],
        compiler_params=_cparams(("parallel", "parallel")),
        name="pool_mixer",
    )(x, x, x, nw, shift, scale, gate, w_grp, pscale)


def _pos_tables(n_tokens, d):
    rows = n_tokens // GRID_WIDTH
    quarter = d // 4
    omega = 1.0 / (POS_BASE ** (jnp.arange(quarter, dtype=F32) / quarter))

    def table(n):
        v = jnp.arange(n, dtype=F32)[:, None] * omega
        return jnp.concatenate([jnp.sin(v), jnp.cos(v)], axis=-1)

    return table(rows), table(GRID_WIDTH)


def _flat(u):
    return u.reshape((1, u.shape[0] * u.shape[1]) + u.shape[2:])


def _unflat(u, b):
    return u.reshape((b, u.shape[0] * u.shape[1] // b) + u.shape[2:])


def _ssd_layer(x, ctx, mods, mods_c, nw, p, with_ctx, pos=None):
    sh1, sc1, g1 = mods[0], mods[1], mods[2]
    ch1, cs1, cg1 = mods_c[0], mods_c[1], mods_c[2]
    d_inner = p["out_proj"].shape[1]
    heads = p["dt_bias"].shape[-1]
    hpg = heads // SSD_GROUPS
    gw = d_inner // SSD_GROUPS
    dt_bias = p["dt_bias"].reshape(1, 2 * heads)
    alog = p["a_log"].reshape(2 * SSD_GROUPS, hpg)
    skip_diag = (p["d"].reshape(2 * SSD_GROUPS, hpg)[:, :, None, None]
               * jnp.eye(SSD_CHUNK, dtype=F32)[None, None]).astype(BF16)

    b = x.shape[0]

    def project(u, shift, scale, u_pos=None):
        zx, dts = _in_proj(u, u_pos, nw, shift, scale, p["in_proj"], p["layer"], dt_bias)
        xbc = _conv_silu(_unflat(zx, b), p["conv_w"], p["conv_b"], d_inner)
        dtc5, dtr5 = _dt_layouts(_unflat(dts, b), hpg)
        return zx, xbc, dtc5, dtr5

    zx_c, xbc_c, dtc_c, dtr_c = project(_flat(ctx), ch1, cs1)
    zx_l, xbc_l, dtc_l, dtr_l = project(x, sh1, sc1, pos)
    h0 = jnp.zeros((b, 2, SSD_GROUPS, SSD_STATE, gw), F32)
    y_c, st_c = _ssd_scan(xbc_c, dtc_c, dtr_c, alog, skip_diag, h0, d_inner)
    y_l, _ = _ssd_scan(xbc_l, dtc_l, dtr_l, alog, skip_diag, st_c, d_inner)
    x = _ssd_out(y_l, zx_l, p["norm_w"], p["out_proj"], p["layer"], x, g1, pos)
    if with_ctx:
        y_flat = y_c.reshape(2, 1, -1, d_inner)
        ctx = _unflat(_ssd_out(y_flat, zx_c, p["norm_w"], p["out_proj"], p["layer"], _flat(ctx), cg1), b)
    return x, ctx


def _fourier_layer(x, ctx, mods, mods_c, nw, w_out, with_ctx):
    d = x.shape[-1]
    gw = d // N_FOURIER_GROUPS
    cc, sc_ = _dft_tables(gw, gw ** -0.5)
    w_cs = jnp.concatenate([cc, sc_], axis=1).astype(BF16)

    def mix(u, shift, scale, gate):
        l = u.shape[1]
        cl, sl = _dft_tables(l, l ** -0.5)
        a_mat = jnp.concatenate([cl, -sl], axis=1).astype(BF16)
        pq = _chan_dft(u, nw, shift, scale, w_cs)
        f = _seq_dft(a_mat, pq.reshape(u.shape[0], 2 * l, d))
        if gate.shape[0] == 1:
            return _unflat(_matmul_residual(_flat(f), w_out, _flat(u), gate), u.shape[0])
        return _matmul_residual(f, w_out, u, gate)

    x = mix(x, mods[0], mods[1], mods[2])
    if with_ctx:
        b = ctx.shape[0]
        ctx = mix(ctx, jnp.broadcast_to(mods_c[0], (b, 1, d)), jnp.broadcast_to(mods_c[1], (b, 1, d)), mods_c[2])
    return x, ctx


def kernel(x, c, ctx, c_ctx, w_mod, b_mod, norm_w, ffn_w_gate, ffn_w_up, ffn_w_down, ssd_in_proj, ssd_conv_w,
           ssd_conv_b, ssd_dt_bias, ssd_a_log, ssd_d, ssd_norm_w, ssd_out_proj, fourier_w_out, pool_w, pool_scale,
           final_norm_w):
    b, l, d = x.shape
    depth = w_mod.shape[0]
    n_ctx_rows = V7X_SUBLANES
    c_all = jnp.concatenate([c, jnp.broadcast_to(c_ctx[None, :], (n_ctx_rows, d))], axis=0)
    mod_all = _modulation(c_all, w_mod, b_mod)

    wg16, wu16, wd16 = ffn_w_gate.astype(BF16), ffn_w_up.astype(BF16), ffn_w_down.astype(BF16)
    in_proj16, out_proj16 = ssd_in_proj.astype(BF16), ssd_out_proj.astype(BF16)

    pos = _pos_tables(l, d)
    for i in range(depth):
        kind, j = i % N_MIXERS, i // N_MIXERS
        last = i == depth - 1
        mods = [mod_all[i, :b, None, k * d:(k + 1) * d] for k in range(6)]
        mods_c = [mod_all[i, b:b + 1, None, k * d:(k + 1) * d] for k in range(6)]
        nw1 = norm_w[i, 0][None, :]
        nw2 = norm_w[i, 1][None, :]
        if kind == 0:
            p = dict(in_proj=in_proj16, out_proj=out_proj16, layer=j, conv_w=ssd_conv_w[j],
                     conv_b=ssd_conv_b[j][None, :], dt_bias=ssd_dt_bias[j], a_log=ssd_a_log[j], d=ssd_d[j],
                     norm_w=ssd_norm_w[j][None, :])
            x, ctx = _ssd_layer(x, ctx, mods, mods_c, nw1, p, not last, pos if i == 0 else None)
        elif kind == 1:
            x, ctx = _fourier_layer(x, ctx, mods, mods_c, nw1, fourier_w_out[j].astype(BF16), not last)
        else:
            pw = pool_w[j].astype(BF16)
            ps = pool_scale[j][None, :]
            x = _pool_mixer(x, nw1, mods[0], mods[1], mods[2], pw, ps)
            if not last:
                ctx = _pool_mixer(ctx, nw1, *[jnp.broadcast_to(m, (b, 1, d)) for m in mods_c[:3]], pw, ps)
        fw = final_norm_w[None, :] if last else None
        x = _ffn(x, nw2, mods[3], mods[4], mods[5], wg16, wu16, wd16, i, fw)
        if not last:
            ctx = _unflat(_ffn(_flat(ctx), nw2, mods_c[3], mods_c[4], mods_c[5], wg16, wu16, wd16, i), b)
    return x
```

```python
import functools
import math

import jax
import jax.numpy as jnp
from jax import lax
from jax.experimental import pallas as pl
from jax.experimental.pallas import tpu as pltpu

F32 = jnp.float32
BF16 = jnp.bfloat16

GRID_WIDTH = 64
POS_BASE = 10000.0
NORM_EPS = 1e-6
SSD_NORM_EPS = 1e-5
SSD_HEAD_DIM = 64
SSD_GROUPS = 8
SSD_STATE = 128
SSD_CHUNK = 128
N_FOURIER_GROUPS = 4
POOL_WINDOWS = (2, 4, 8, 16)
N_MIXERS = 3
SCAN_CHUNKS_PER_STEP = 32

V7X_LANES = 128
V7X_SUBLANES = 8
V7X_VMEM_BYTES = 64 * 1024 * 1024
V7X_VMEM_LIMIT_BYTES = V7X_VMEM_BYTES - 8 * 1024 * 1024
HALO_ROWS = V7X_SUBLANES
NEG_BIG = -1e30
LOG2_E = 1.4426950408889634


def _cparams(semantics):
    return pltpu.CompilerParams(dimension_semantics=semantics,
                                vmem_limit_bytes=V7X_VMEM_LIMIT_BYTES)


def _tile(n, pref):
    t = min(n, pref)
    assert n % t == 0, (n, t)
    return t


def _silu(v):
    h = 0.5 * v
    return h + h * jnp.tanh(h)


def _ada_norm(x, nw, shift, scale):
    ms = jnp.mean(x * x, axis=-1, keepdims=True)
    y = x * lax.rsqrt(ms + NORM_EPS)
    return (y * nw) * (1.0 + scale) + shift


def _pos_tile(rt_ref, ct_ref, tile_idx, t):
    w = GRID_WIDTH
    n_rows = t // w
    half = rt_ref.shape[1]
    r0 = tile_idx * n_rows
    left = jnp.concatenate([jnp.broadcast_to(rt_ref[pl.ds(r0 + rr, 1), :], (w, half)) for rr in range(n_rows)],
                           axis=0)
    right = jnp.concatenate([ct_ref[...]] * n_rows, axis=0)
    return jnp.concatenate([left, right], axis=1)


def _split3(v):
    hi = v.astype(BF16)
    r1 = v - hi.astype(F32)
    mid = r1.astype(BF16)
    lo = (r1 - mid.astype(F32)).astype(BF16)
    return hi, mid, lo


def _dot(a, b):
    return jnp.dot(a, b, preferred_element_type=F32)


def _mod_kernel(c_ref, w_ref, b_ref, o_ref):
    s = _silu(c_ref[...])
    o_ref[...] = jnp.dot(s, w_ref[...], preferred_element_type=F32,
                         precision=lax.Precision.HIGHEST) + b_ref[...]


def _modulation(c_all, w_mod, b_mod):
    depth, d, n = w_mod.shape
    rows = c_all.shape[0]
    tn = _tile(n, 1024)
    return pl.pallas_call(
        _mod_kernel,
        grid=(depth, n // tn),
        in_specs=[
            pl.BlockSpec((rows, d), lambda i, j: (0, 0)),
            pl.BlockSpec((None, d, tn), lambda i, j: (i, 0, j)),
            pl.BlockSpec((None, 1, tn), lambda i, j: (i, 0, j)),
        ],
        out_specs=pl.BlockSpec((None, rows, tn), lambda i, j: (i, 0, j)),
        out_shape=jax.ShapeDtypeStruct((depth, rows, n), F32),
        compiler_params=_cparams(("parallel", "parallel")),
        name="modulation",
    )(c_all, w_mod, b_mod.reshape(depth, 1, n))


def _in_proj_kernel(x_ref, *rest, with_pos):
    if with_pos:
        rt_ref, ct_ref, nw_ref, sh_ref, sc_ref, w_ref, wdt_ref, b_ref, o_ref, dt_ref, h_ref = rest
    else:
        nw_ref, sh_ref, sc_ref, w_ref, wdt_ref, b_ref, o_ref, dt_ref, h_ref = rest

    @pl.when(pl.program_id(2) == 0)
    def _():
        xv = x_ref[...]
        if with_pos:
            xv = xv + _pos_tile(rt_ref, ct_ref, pl.program_id(1), xv.shape[0])
        h = _ada_norm(xv, nw_ref[...], sh_ref[...], sc_ref[...]).astype(BF16)
        h_ref[...] = h
        v = _dot(h, wdt_ref[...]) + b_ref[...]
        dt_ref[...] = jnp.maximum(v, 0.0) + jnp.log1p(jnp.exp(-jnp.abs(v)))

    o_ref[...] = _dot(h_ref[...], w_ref[...]).astype(o_ref.dtype)


def _in_proj(x, pos, nw, shift, scale, w, layer, dt_bias):
    b, l, d = x.shape
    ndt = dt_bias.shape[1]
    n = w.shape[2] - ndt
    assert n % ndt == 0
    dt_block = n // ndt
    t = _tile(l, 1024)
    with_pos = pos is not None
    tn = _tile(n, 1024)
    pos_args = list(pos) if with_pos else []
    pos_specs = [pl.BlockSpec(a.shape, lambda i, j, k: (0, 0)) for a in pos_args]
    assert not with_pos or t % GRID_WIDTH == 0
    return pl.pallas_call(
        functools.partial(_in_proj_kernel, with_pos=with_pos),
        grid=(b, l // t, n // tn),
        in_specs=[pl.BlockSpec((None, t, d), lambda i, j, k: (i, j, 0))] + pos_specs + [
            pl.BlockSpec((1, d), lambda i, j, k: (0, 0)),
            pl.BlockSpec((None, 1, d), lambda i, j, k: (i, 0, 0)),
            pl.BlockSpec((None, 1, d), lambda i, j, k: (i, 0, 0)),
            pl.BlockSpec((None, d, tn), lambda i, j, k: (layer, 0, k)),
            pl.BlockSpec((None, d, ndt), lambda i, j, k: (layer, 0, dt_block)),
            pl.BlockSpec((1, ndt), lambda i, j, k: (0, 0)),
        ],
        out_specs=[pl.BlockSpec((None, t, tn), lambda i, j, k: (i, j, k)),
                   pl.BlockSpec((None, t, ndt), lambda i, j, k: (i, j, 0))],
        out_shape=[jax.ShapeDtypeStruct((b, l, n), BF16), jax.ShapeDtypeStruct((b, l, ndt), F32)],
        scratch_shapes=[pltpu.VMEM((t, d), BF16)],
        compiler_params=_cparams(("parallel", "parallel", "arbitrary")),
        name="in_proj",
    )(x, *pos_args, nw, shift, scale, w, w, dt_bias)


def _conv_kernel(cur_ref, prev_ref, next_ref, sel_ref, w_ref, b_ref, o_ref, *, t, n_tiles, width):
    l = pl.program_id(1)
    pad = width // 2
    halo = prev_ref.shape[0]
    sb = sel_ref.shape[1]
    zero = jnp.zeros(prev_ref.shape, prev_ref.dtype)
    prev = jnp.where(l > 0, prev_ref[...], zero)
    nxt = jnp.where(l < n_tiles - 1, next_ref[...], zero)
    ext = jnp.concatenate([prev, cur_ref[...], nxt], axis=0)
    for s in range(t // sb):
        win = ext[s * sb:(s + 1) * sb + 2 * halo]
        acc = b_ref[...] + w_ref[pad:pad + 1, :] * win[halo:halo + sb].astype(F32)
        for k in range(width):
            if k != pad:
                acc = acc + w_ref[k:k + 1, :] * _dot(sel_ref[k], win)
        o_ref[s * sb:(s + 1) * sb, :] = _silu(acc).astype(o_ref.dtype)


def _conv_silu(zx, conv_w, conv_b, col_offset):
    b, l, _ = zx.shape
    width, c = conv_w.shape
    pad = width // 2
    halo = 2 * V7X_SUBLANES
    assert pad <= halo
    t = _tile(l, 512)
    tc = _tile(c, 2048)
    sb = _tile(t, V7X_LANES)
    n_tiles = l // t
    off = col_offset // tc
    hb = t // halo
    last_hb = l // halo - 1
    i_idx = jnp.arange(sb)[None, :, None]
    j_idx = jnp.arange(sb + 2 * halo)[None, None, :]
    k_idx = jnp.arange(width)[:, None, None]
    sel = (j_idx == i_idx + halo + k_idx - pad).astype(BF16)
    return pl.pallas_call(
        functools.partial(_conv_kernel, t=t, n_tiles=n_tiles, width=width),
        grid=(b, n_tiles, c // tc),
        in_specs=[
            pl.BlockSpec((None, t, tc), lambda i, j, k: (i, j, off + k)),
            pl.BlockSpec((None, halo, tc), lambda i, j, k: (i, jnp.maximum(j * hb - 1, 0), off + k)),
            pl.BlockSpec((None, halo, tc), lambda i, j, k: (i, jnp.minimum((j + 1) * hb, last_hb), off + k)),
            pl.BlockSpec((width, sb, sb + 2 * halo), lambda i, j, k: (0, 0, 0)),
            pl.BlockSpec((width, tc), lambda i, j, k: (0, k)),
            pl.BlockSpec((1, tc), lambda i, j, k: (0, k)),
        ],
        out_specs=pl.BlockSpec((None, t, tc), lambda i, j, k: (i, j, k)),
        out_shape=jax.ShapeDtypeStruct((b, l, c), BF16),
        compiler_params=_cparams(("parallel", "parallel", "parallel")),
        name="conv_silu",
    )(zx, zx, zx, sel, conv_w, conv_b)


def _scan_kernel(xs_ref, b_ref, c_ref, dtc_ref, dtr_ref, alr_ref, alc_ref, skd_ref, h0_ref,
                 tm_ref, tmt_ref, madd_ref, y_ref, st_ref, *, cps, heads):
    d = pl.program_id(1)
    s = pl.program_id(3)
    q = SSD_CHUNK
    p = SSD_HEAD_DIM

    @pl.when(s == 0)
    def _():
        st_ref[...] = h0_ref[...]

    lane = lax.broadcasted_iota(jnp.int32, (q, 2 * p), 1)
    lo_half = lane < p
    tm = tm_ref[...]
    tm_t = tmt_ref[...]

    dtr = dtr_ref[...]
    la_c = dtc_ref[...] * (-jnp.exp(alr_ref[...]) * LOG2_E)
    la_r = dtr * (-jnp.exp(alc_ref[...]) * LOG2_E)
    c_hi, c_mid, c_lo = _split3(la_c)
    cum_c = _dot(tm, c_hi) + _dot(tm, c_mid) + _dot(tm, c_lo)
    r_hi, r_mid, r_lo = _split3(la_r)
    cum_r = _dot(r_hi, tm_t) + _dot(r_mid, tm_t) + _dot(r_lo, tm_t)
    tot_r = jnp.sum(la_r, axis=1, keepdims=True)
    ldt_r = jnp.log2(dtr)
    f_r = jnp.exp2(tot_r - cum_r + ldt_r)
    rowb_all = cum_r - ldt_r

    st = st_ref[...]
    for i in range(cps):
        j = i + d * (cps - 1 - 2 * i)
        off = pl.multiple_of(j * q, q)
        x16 = xs_ref[pl.ds(off, q), :]
        b16 = b_ref[pl.ds(off, q), :]
        c16 = c_ref[pl.ds(off, q), :]
        bt16 = b16.astype(F32).T.astype(BF16)
        scores16 = _dot(c16, bt16).astype(BF16)
        y_state = _dot(c16, st.astype(BF16))

        y_parts, upd_parts, ec_parts = [], [], []
        for pair in range(heads // 2):
            ms, bs, ecs = [], [], []
            for e in (2 * pair, 2 * pair + 1):
                k = i * heads + e
                colb = jnp.broadcast_to(cum_c[:, k:k + 1], (q, q))
                dec = jnp.exp2((colb - rowb_all[k:k + 1, :]) + madd_ref[...])
                ms.append(scores16 * dec.astype(BF16) + skd_ref[e])
                bs.append(bt16 * jnp.broadcast_to(f_r[k:k + 1, :], (q, q)).astype(BF16))
                ecs.append(jnp.exp2(colb))
            xp = x16[:, pair * 2 * p:(pair + 1) * 2 * p]
            zero = jnp.zeros_like(xp)
            rhs = jnp.concatenate([jnp.where(lo_half, xp, zero), jnp.where(lo_half, zero, xp)], axis=0)
            y_parts.append(_dot(jnp.concatenate(ms, axis=1), rhs))
            upd_parts.append(_dot(jnp.concatenate(bs, axis=1), rhs))
            ec_parts.append(jnp.where(lo_half, ecs[0], ecs[1]))
        ec = jnp.concatenate(ec_parts, axis=1)
        y = jnp.concatenate(y_parts, axis=1) + y_state * ec
        y_ref[pl.ds(off, q), :] = y.astype(y_ref.dtype)
        etot = jnp.where(d == 0, ec[q - 1:q, :], ec[0:1, :])
        st = st * etot + jnp.concatenate(upd_parts, axis=1)
    st_ref[...] = st


def _scan_masks():
    q = SSD_CHUNK
    row = jnp.arange(q)[:, None]
    col = jnp.arange(q)[None, :]
    tri = jnp.stack([row >= col, row <= col])
    return (tri.astype(BF16), jnp.swapaxes(tri, 1, 2).astype(BF16),
            jnp.where(tri, 0.0, NEG_BIG).astype(F32))


def _scan_chunks_per_step(l):
    return min(l // SSD_CHUNK, SCAN_CHUNKS_PER_STEP)


def _dt_layouts(dts, heads):
    b, l, _ = dts.shape
    cps = _scan_chunks_per_step(l)
    steps = l // (cps * SSD_CHUNK)
    v = dts.reshape(b, steps, cps, SSD_CHUNK, 2, SSD_GROUPS, heads)
    v = jnp.stack([v[:, :, :, :, 0], v[:, :, ::-1, :, 1]], axis=1)
    col = v.transpose(0, 1, 5, 2, 4, 3, 6).reshape(b, 2 * SSD_GROUPS, steps, SSD_CHUNK, cps * heads)
    row = v.transpose(0, 1, 5, 2, 3, 6, 4).reshape(b, 2 * SSD_GROUPS, steps, cps * heads, SSD_CHUNK)
    return col, row


def _ssd_scan(xbc, dtc5, dtr5, alog, skip_diag, h0, d_inner):
    b, l, _ = xbc.shape
    g = SSD_GROUPS
    n = SSD_STATE
    gw = d_inner // g
    heads = gw // SSD_HEAD_DIM
    cps = _scan_chunks_per_step(l)
    r = cps * SSD_CHUNK
    steps = l // r
    b_off = d_inner // n
    c_off = (d_inner + g * n) // n
    alog_t = jnp.tile(alog, (1, cps))
    alog_row = alog_t[:, None, :]
    alog_col = alog_t[:, :, None]

    def cb(si, di):
        return si + di * (steps - 1 - 2 * si)

    tm, tm_t, madd = _scan_masks()
    mask_spec = pl.BlockSpec((None, SSD_CHUNK, SSD_CHUNK), lambda bi, di, gi, si: (di, 0, 0))
    return pl.pallas_call(
        functools.partial(_scan_kernel, cps=cps, heads=heads),
        grid=(b, 2, g, steps),
        in_specs=[
            pl.BlockSpec((None, r, gw), lambda bi, di, gi, si: (bi, cb(si, di), gi)),
            pl.BlockSpec((None, r, n), lambda bi, di, gi, si: (bi, cb(si, di), b_off + gi)),
            pl.BlockSpec((None, r, n), lambda bi, di, gi, si: (bi, cb(si, di), c_off + gi)),
            pl.BlockSpec((None, None, None, SSD_CHUNK, cps * heads),
                         lambda bi, di, gi, si: (bi, di * g + gi, cb(si, di), 0, 0)),
            pl.BlockSpec((None, None, None, cps * heads, SSD_CHUNK),
                         lambda bi, di, gi, si: (bi, di * g + gi, cb(si, di), 0, 0)),
            pl.BlockSpec((None, 1, cps * heads), lambda bi, di, gi, si: (di * g + gi, 0, 0)),
            pl.BlockSpec((None, cps * heads, 1), lambda bi, di, gi, si: (di * g + gi, 0, 0)),
            pl.BlockSpec((None, heads, SSD_CHUNK, SSD_CHUNK), lambda bi, di, gi, si: (di * g + gi, 0, 0, 0)),
            pl.BlockSpec((None, None, None, n, gw), lambda bi, di, gi, si: (bi, di, gi, 0, 0)),
            mask_spec, mask_spec, mask_spec,
        ],
        out_specs=[
            pl.BlockSpec((None, None, r, gw), lambda bi, di, gi, si: (di, bi, cb(si, di), gi)),
            pl.BlockSpec((None, None, None, n, gw), lambda bi, di, gi, si: (bi, di, gi, 0, 0)),
        ],
        out_shape=[jax.ShapeDtypeStruct((2, b, l, d_inner), BF16),
                   jax.ShapeDtypeStruct((b, 2, g, n, gw), F32)],
        compiler_params=_cparams(("parallel", "parallel", "parallel", "arbitrary")),
        name="ssd_scan",
    )(xbc, xbc, xbc, dtc5, dtr5, alog_row, alog_col, skip_diag, h0, tm, tm_t, madd)


def _ssd_out_kernel(yf_ref, yb_ref, z_ref, nw_ref, w_ref, x_ref, g_ref, *rest, gw, with_pos):
    if with_pos:
        rt_ref, ct_ref, o_ref, a_ref = rest
    else:
        o_ref, a_ref = rest
    for gi in range(a_ref.shape[1] // gw):
        sl = slice(gi * gw, (gi + 1) * gw)
        v = (yf_ref[:, sl].astype(F32) + yb_ref[:, sl].astype(F32)) * _silu(z_ref[:, sl].astype(F32))
        ms = jnp.mean(v * v, axis=-1, keepdims=True)
        a_ref[:, sl] = (v * lax.rsqrt(ms + SSD_NORM_EPS) * nw_ref[:, sl]).astype(BF16)
    xv = x_ref[...]
    if with_pos:
        xv = xv + _pos_tile(rt_ref, ct_ref, pl.program_id(1), xv.shape[0])
    o_ref[...] = xv + g_ref[...] * _dot(a_ref[...], w_ref[...])


def _ssd_out(y2, zx, norm_w, w_out, layer, x, gate, pos=None):
    b, l, d = x.shape
    k = w_out.shape[1]
    t = _tile(l, 256)
    with_pos = pos is not None
    pos_args = list(pos) if with_pos else []
    pos_specs = [pl.BlockSpec(a.shape, lambda i, j: (0, 0)) for a in pos_args]
    assert not with_pos or t % GRID_WIDTH == 0
    return pl.pallas_call(
        functools.partial(_ssd_out_kernel, gw=k // SSD_GROUPS, with_pos=with_pos),
        grid=(b, l // t),
        in_specs=[
            pl.BlockSpec((None, None, t, k), lambda i, j: (0, i, j, 0)),
            pl.BlockSpec((None, None, t, k), lambda i, j: (1, i, j, 0)),
            pl.BlockSpec((None, t, k), lambda i, j: (i, j, 0)),
            pl.BlockSpec((1, k), lambda i, j: (0, 0)),
            pl.BlockSpec((None, k, d), lambda i, j: (layer, 0, 0), pipeline_mode=pl.Buffered(1)),
            pl.BlockSpec((None, t, d), lambda i, j: (i, j, 0)),
            pl.BlockSpec((None, 1, d), lambda i, j: (i, 0, 0)),
        ] + pos_specs,
        out_specs=pl.BlockSpec((None, t, d), lambda i, j: (i, j, 0)),
        out_shape=jax.ShapeDtypeStruct(x.shape, F32),
        scratch_shapes=[pltpu.VMEM((t, k), BF16)],
        compiler_params=_cparams(("parallel", "parallel")),
        name="ssd_out",
    )(y2, y2, zx, norm_w, w_out, x, gate, *pos_args)


def _mm_res_kernel(a_ref, w_ref, x_ref, g_ref, o_ref):
    o_ref[...] = x_ref[...] + g_ref[...] * _dot(a_ref[...], w_ref[...])


def _matmul_residual(a, w, x, gate):
    b, l, d = x.shape
    k = w.shape[0]
    t = _tile(l, 1024)
    tn = _tile(d, 1024)
    return pl.pallas_call(
        _mm_res_kernel,
        grid=(b, l // t, d // tn),
        in_specs=[
            pl.BlockSpec((None, t, k), lambda i, j, n: (i, j, 0)),
            pl.BlockSpec((k, tn), lambda i, j, n: (0, n)),
            pl.BlockSpec((None, t, tn), lambda i, j, n: (i, j, n)),
            pl.BlockSpec((None, 1, tn), lambda i, j, n: (i, 0, n)),
        ],
        out_specs=pl.BlockSpec((None, t, tn), lambda i, j, n: (i, j, n)),
        out_shape=jax.ShapeDtypeStruct(x.shape, F32),
        compiler_params=_cparams(("parallel", "parallel", "parallel")),
        name="matmul_residual",
    )(a, w, x, gate)


def _ffn_kernel(x_ref, nw_ref, sh_ref, sc_ref, g_ref, wg_ref, wu_ref, wd_ref, *rest, final):
    if final:
        fw_ref, o_ref, h_ref = rest
    else:
        o_ref, h_ref = rest
    j = pl.program_id(2)

    @pl.when(j == 0)
    def _():
        h_ref[...] = _ada_norm(x_ref[...], nw_ref[...], sh_ref[...], sc_ref[...]).astype(BF16)
        o_ref[...] = jnp.zeros_like(o_ref)

    h = h_ref[...]
    a = _dot(h, wg_ref[...])
    u = _dot(h, wu_ref[...])
    o_ref[...] += _dot((_silu(a) * u).astype(BF16), wd_ref[...])

    @pl.when(j == pl.num_programs(2) - 1)
    def _():
        out = x_ref[...] + g_ref[...] * o_ref[...]
        if final:
            ms = jnp.mean(out * out, axis=-1, keepdims=True)
            out = out * lax.rsqrt(ms + NORM_EPS) * fw_ref[...]
        o_ref[...] = out


def _ffn(x, nw, shift, scale, gate, wg, wu, wd, layer, final_w=None):
    b, l, d = x.shape
    hdim = wg.shape[2]
    t = _tile(l, 512)
    th = _tile(hdim, 512)
    final = final_w is not None
    vec = pl.BlockSpec((None, 1, d), lambda i, j, k: (i, 0, 0))
    in_specs = [
        pl.BlockSpec((None, t, d), lambda i, j, k: (i, j, 0)),
        pl.BlockSpec((1, d), lambda i, j, k: (0, 0)),
        vec, vec, vec,
        pl.BlockSpec((None, d, th), lambda i, j, k: (layer, 0, k)),
        pl.BlockSpec((None, d, th), lambda i, j, k: (layer, 0, k)),
        pl.BlockSpec((None, th, d), lambda i, j, k: (layer, k, 0)),
    ]
    args = [x, nw, shift, scale, gate, wg, wu, wd]
    if final:
        in_specs.append(pl.BlockSpec((1, d), lambda i, j, k: (0, 0)))
        args.append(final_w)
    return pl.pallas_call(
        functools.partial(_ffn_kernel, final=final),
        grid=(b, l // t, hdim // th),
        in_specs=in_specs,
        out_specs=pl.BlockSpec((None, t, d), lambda i, j, k: (i, j, 0)),
        out_shape=jax.ShapeDtypeStruct(x.shape, F32),
        scratch_shapes=[pltpu.VMEM((t, d), BF16)],
        compiler_params=_cparams(("parallel", "parallel", "arbitrary")),
        name="ffn",
    )(*args)


def _chan_dft_kernel(x_ref, nw_ref, sh_ref, sc_ref, w_ref, o_ref, *, gw):
    h = _ada_norm(x_ref[...], nw_ref[...], sh_ref[...], sc_ref[...]).astype(BF16)
    for gi in range(h.shape[1] // gw):
        sl = slice(gi * gw, (gi + 1) * gw)
        r = _dot(h[:, sl], w_ref[...])
        o_ref[0, :, sl] = r[:, :gw].astype(BF16)
        o_ref[1, :, sl] = r[:, gw:].astype(BF16)


def _chan_dft(x, nw, shift, scale, w_cs):
    b, l, d = x.shape
    gw = w_cs.shape[0]
    t = _tile(l, 512)
    vec = pl.BlockSpec((None, 1, d), lambda i, j: (i, 0, 0))
    return pl.pallas_call(
        functools.partial(_chan_dft_kernel, gw=gw),
        grid=(b, l // t),
        in_specs=[
            pl.BlockSpec((None, t, d), lambda i, j: (i, j, 0)),
            pl.BlockSpec((1, d), lambda i, j: (0, 0)),
            vec, vec,
            pl.BlockSpec((gw, 2 * gw), lambda i, j: (0, 0)),
        ],
        out_specs=pl.BlockSpec((None, 2, t, d), lambda i, j: (i, 0, j, 0)),
        out_shape=jax.ShapeDtypeStruct((b, 2, l, d), BF16),
        compiler_params=_cparams(("parallel", "parallel")),
        name="chan_dft",
    )(x, nw, shift, scale, w_cs)


def _seq_dft_kernel(a_ref, pq_ref, o_ref, acc_ref):
    k = pl.program_id(3)

    @pl.when(k == 0)
    def _():
        acc_ref[...] = jnp.zeros_like(acc_ref)

    acc_ref[...] += _dot(a_ref[...], pq_ref[...])

    @pl.when(k == pl.num_programs(3) - 1)
    def _():
        o_ref[...] = acc_ref[...].astype(o_ref.dtype)


def _seq_dft(a_mat, pq):
    b, k2, d = pq.shape
    l = a_mat.shape[0]
    tm = _tile(l, 1024)
    tn = _tile(d, 1024)
    tk = _tile(k2, 2048)
    return pl.pallas_call(
        _seq_dft_kernel,
        grid=(b, l // tm, d // tn, k2 // tk),
        in_specs=[
            pl.BlockSpec((tm, tk), lambda i, m, n, k: (m, k)),
            pl.BlockSpec((None, tk, tn), lambda i, m, n, k: (i, k, n)),
        ],
        out_specs=pl.BlockSpec((None, tm, tn), lambda i, m, n, k: (i, m, n)),
        out_shape=jax.ShapeDtypeStruct((b, l, d), BF16),
        scratch_shapes=[pltpu.VMEM((tm, tn), F32)],
        compiler_params=_cparams(("parallel", "parallel", "parallel", "arbitrary")),
        name="seq_dft",
    )(a_mat, pq)


def _dft_tables(n, scale):
    idx = jnp.arange(n, dtype=jnp.int32)

    def cs(prod):
        ang = (prod % n).astype(F32) * (2.0 * math.pi / n)
        return jnp.cos(ang), jnp.sin(ang)

    m = GRID_WIDTH
    if n <= m or n % m:
        c, s = cs(idx[:, None] * idx[None, :])
        return c * scale, s * scale
    ca, sa = cs(idx[:, None] * (m * jnp.arange(n // m, dtype=jnp.int32))[None, :])
    cb, sb = cs(idx[:, None] * jnp.arange(m, dtype=jnp.int32)[None, :])
    ca, sa = ca[:, :, None] * scale, sa[:, :, None] * scale
    cb, sb = cb[:, None, :], sb[:, None, :]
    return (ca * cb - sa * sb).reshape(n, n), (sa * cb + ca * sb).reshape(n, n)


def _pool_kernel(cur_ref, prev_ref, next_ref, nw_ref, sh_ref, sc_ref, g_ref, w_ref, ps_ref,
                 o_ref, ext_ref, *, t, n_tiles, seq_len, gw):
    l = pl.program_id(1)
    nw, sh, sc = nw_ref[...], sh_ref[...], sc_ref[...]
    hc = _ada_norm(cur_ref[...], nw, sh, sc)
    ext_ref[0:HALO_ROWS, :] = jnp.where(l > 0, _ada_norm(prev_ref[...], nw, sh, sc), 0.0)
    ext_ref[HALO_ROWS:HALO_ROWS + t, :] = hc
    ext_ref[HALO_ROWS + t:, :] = jnp.where(l < n_tiles - 1, _ada_norm(next_ref[...], nw, sh, sc), 0.0)
    pos = l * t + lax.broadcasted_iota(jnp.int32, (t, 1), 0)
    for gi, win in enumerate(POOL_WINDOWS):
        sl = slice(gi * gw, (gi + 1) * gw)
        half = win // 2
        tot = ext_ref[pl.ds(HALO_ROWS - half, t), sl]
        for k in range(-half + 1, win - half):
            tot = tot + ext_ref[pl.ds(HALO_ROWS + k, t), sl]
        cnt = (jnp.minimum(pos + (win - half), seq_len) - jnp.maximum(pos - half, 0)).astype(F32)
        pooled = (tot / cnt - hc[:, sl]).astype(BF16)
        out = _dot(pooled, w_ref[gi]) * ps_ref[:, sl]
        o_ref[:, sl] = cur_ref[:, sl] + g_ref[:, sl] * out


def _pool_mixer(x, nw, shift, scale, gate, w_grp, pscale):
    b, l, d = x.shape
    ng, gw, _ = w_grp.shape
    assert max(POOL_WINDOWS) // 2 <= HALO_ROWS
    t = _tile(l, 512)
    n_tiles = l // t
    rb = t // HALO_ROWS
    last_rb = l // HALO_ROWS - 1
    vec = pl.BlockSpec((None, 1, d), lambda i, j: (i, 0, 0))
    return pl.pallas_call(
        functools.partial(_pool_kernel, t=t, n_tiles=n_tiles, seq_len=l, gw=gw),
        grid=(b, n_tiles),
        in_specs=[
            pl.BlockSpec((None, t, d), lambda i, j: (i, j, 0)),
            pl.BlockSpec((None, HALO_ROWS, d), lambda i, j: (i, jnp.maximum(j * rb - 1, 0), 0)),
            pl.BlockSpec((None, HALO_ROWS, d), lambda i, j: (i, jnp.minimum((j + 1) * rb, last_rb), 0)),
            pl.BlockSpec((1, d), lambda i, j: (0, 0)),
            vec, vec, vec,
            pl.BlockSpec((ng, gw, gw), lambda i, j: (0, 0, 0)),
            pl.BlockSpec((1, d), lambda i, j: (0, 0)),
        ],
        out_specs=pl.BlockSpec((None, t, d), lambda i, j: (i, j, 0)),
        out_shape=jax.ShapeDtypeStruct(x.shape, F32),
        scratch_shapes=[pltpu.VMEM((t + 2 * HALO_ROWS, d), F32)],
        compiler_params=_cparams(("parallel", "parallel")),
        name="pool_mixer",
    )(x, x, x, nw, shift, scale, gate, w_grp, pscale)


def _pos_tables(n_tokens, d):
    rows = n_tokens // GRID_WIDTH
    quarter = d // 4
    omega = 1.0 / (POS_BASE ** (jnp.arange(quarter, dtype=F32) / quarter))

    def table(n):
        v = jnp.arange(n, dtype=F32)[:, None] * omega
        return jnp.concatenate([jnp.sin(v), jnp.cos(v)], axis=-1)

    return table(rows), table(GRID_WIDTH)


def _flat(u):
    return u.reshape((1, u.shape[0] * u.shape[1]) + u.shape[2:])


def _unflat(u, b):
    return u.reshape((b, u.shape[0] * u.shape[1] // b) + u.shape[2:])


def _ssd_layer(x, ctx, mods, mods_c, nw, p, with_ctx, pos=None):
    sh1, sc1, g1 = mods[0], mods[1], mods[2]
    ch1, cs1, cg1 = mods_c[0], mods_c[1], mods_c[2]
    d_inner = p["out_proj"].shape[1]
    heads = p["dt_bias"].shape[-1]
    hpg = heads // SSD_GROUPS
    gw = d_inner // SSD_GROUPS
    dt_bias = p["dt_bias"].reshape(1, 2 * heads)
    alog = p["a_log"].reshape(2 * SSD_GROUPS, hpg)
    skip_diag = (p["d"].reshape(2 * SSD_GROUPS, hpg)[:, :, None, None]
               * jnp.eye(SSD_CHUNK, dtype=F32)[None, None]).astype(BF16)

    b = x.shape[0]

    def project(u, shift, scale, u_pos=None):
        zx, dts = _in_proj(u, u_pos, nw, shift, scale, p["in_proj"], p["layer"], dt_bias)
        xbc = _conv_silu(_unflat(zx, b), p["conv_w"], p["conv_b"], d_inner)
        dtc5, dtr5 = _dt_layouts(_unflat(dts, b), hpg)
        return zx, xbc, dtc5, dtr5

    zx_c, xbc_c, dtc_c, dtr_c = project(_flat(ctx), ch1, cs1)
    zx_l, xbc_l, dtc_l, dtr_l = project(x, sh1, sc1, pos)
    h0 = jnp.zeros((b, 2, SSD_GROUPS, SSD_STATE, gw), F32)
    y_c, st_c = _ssd_scan(xbc_c, dtc_c, dtr_c, alog, skip_diag, h0, d_inner)
    y_l, _ = _ssd_scan(xbc_l, dtc_l, dtr_l, alog, skip_diag, st_c, d_inner)
    x = _ssd_out(y_l, zx_l, p["norm_w"], p["out_proj"], p["layer"], x, g1, pos)
    if with_ctx:
        y_flat = y_c.reshape(2, 1, -1, d_inner)
        ctx = _unflat(_ssd_out(y_flat, zx_c, p["norm_w"], p["out_proj"], p["layer"], _flat(ctx), cg1), b)
    return x, ctx


def _fourier_layer(x, ctx, mods, mods_c, nw, w_out, with_ctx):
    d = x.shape[-1]
    gw = d // N_FOURIER_GROUPS
    cc, sc_ = _dft_tables(gw, gw ** -0.5)
    w_cs = jnp.concatenate([cc, sc_], axis=1).astype(BF16)

    def mix(u, shift, scale, gate):
        l = u.shape[1]
        cl, sl = _dft_tables(l, l ** -0.5)
        a_mat = jnp.concatenate([cl, -sl], axis=1).astype(BF16)
        pq = _chan_dft(u, nw, shift, scale, w_cs)
        f = _seq_dft(a_mat, pq.reshape(u.shape[0], 2 * l, d))
        if gate.shape[0] == 1:
            return _unflat(_matmul_residual(_flat(f), w_out, _flat(u), gate), u.shape[0])
        return _matmul_residual(f, w_out, u, gate)

    x = mix(x, mods[0], mods[1], mods[2])
    if with_ctx:
        b = ctx.shape[0]
        ctx = mix(ctx, jnp.broadcast_to(mods_c[0], (b, 1, d)), jnp.broadcast_to(mods_c[1], (b, 1, d)), mods_c[2])
    return x, ctx


def kernel(x, c, ctx, c_ctx, w_mod, b_mod, norm_w, ffn_w_gate, ffn_w_up, ffn_w_down, ssd_in_proj, ssd_conv_w,
           ssd_conv_b, ssd_dt_bias, ssd_a_log, ssd_d, ssd_norm_w, ssd_out_proj, fourier_w_out, pool_w, pool_scale,
           final_norm_w):
    b, l, d = x.shape
    depth = w_mod.shape[0]
    n_ctx_rows = V7X_SUBLANES
    c_all = jnp.concatenate([c, jnp.broadcast_to(c_ctx[None, :], (n_ctx_rows, d))], axis=0)
    mod_all = _modulation(c_all, w_mod, b_mod)

    wg16, wu16, wd16 = ffn_w_gate.astype(BF16), ffn_w_up.astype(BF16), ffn_w_down.astype(BF16)
    in_proj16, out_proj16 = ssd_in_proj.astype(BF16), ssd_out_proj.astype(BF16)

    pos = _pos_tables(l, d)
    for i in range(depth):
        kind, j = i % N_MIXERS, i // N_MIXERS
        last = i == depth - 1
        mods = [mod_all[i, :b, None, k * d:(k + 1) * d] for k in range(6)]
        mods_c = [mod_all[i, b:b + 1, None, k * d:(k + 1) * d] for k in range(6)]
        nw1 = norm_w[i, 0][None, :]
        nw2 = norm_w[i, 1][None, :]
        if kind == 0:
            p = dict(in_proj=in_proj16, out_proj=out_proj16, layer=j, conv_w=ssd_conv_w[j],
                     conv_b=ssd_conv_b[j][None, :], dt_bias=ssd_dt_bias[j], a_log=ssd_a_log[j], d=ssd_d[j],
                     norm_w=ssd_norm_w[j][None, :])
            x, ctx = _ssd_layer(x, ctx, mods, mods_c, nw1, p, not last, pos if i == 0 else None)
        elif kind == 1:
            x, ctx = _fourier_layer(x, ctx, mods, mods_c, nw1, fourier_w_out[j].astype(BF16), not last)
        else:
            pw = pool_w[j].astype(BF16)
            ps = pool_scale[j][None, :]
            x = _pool_mixer(x, nw1, mods[0], mods[1], mods[2], pw, ps)
            if not last:
                ctx = _pool_mixer(ctx, nw1, *[jnp.broadcast_to(m, (b, 1, d)) for m in mods_c[:3]], pw, ps)
        fw = final_norm_w[None, :] if last else None
        x = _ffn(x, nw2, mods[3], mods[4], mods[5], wg16, wu16, wd16, i, fw)
        if not last:
            ctx = _unflat(_ffn(_flat(ctx), nw2, mods_c[3], mods_c[4], mods_c[5], wg16, wu16, wd16, i), b)
    return x
```

```python
import functools
import math

import jax
import jax.numpy as jnp
from jax import lax
from jax.experimental import pallas as pl
from jax.experimental.pallas import tpu as pltpu

F32 = jnp.float32
BF16 = jnp.bfloat16

GRID_WIDTH = 64
POS_BASE = 10000.0
NORM_EPS = 1e-6
SSD_NORM_EPS = 1e-5
SSD_HEAD_DIM = 64
SSD_GROUPS = 8
SSD_STATE = 128
SSD_CHUNK = 128
N_FOURIER_GROUPS = 4
POOL_WINDOWS = (2, 4, 8, 16)
N_MIXERS = 3
SCAN_CHUNKS_PER_STEP = 32

V7X_LANES = 128
V7X_SUBLANES = 8
V7X_VMEM_BYTES = 64 * 1024 * 1024
V7X_VMEM_LIMIT_BYTES = V7X_VMEM_BYTES - 8 * 1024 * 1024
HALO_ROWS = V7X_SUBLANES
NEG_BIG = -1e30
LOG2_E = 1.4426950408889634


def _cparams(semantics):
    return pltpu.CompilerParams(dimension_semantics=semantics,
                                vmem_limit_bytes=V7X_VMEM_LIMIT_BYTES)


def _tile(n, pref):
    t = min(n, pref)
    assert n % t == 0, (n, t)
    return t


def _silu(v):
    h = 0.5 * v
    return h + h * jnp.tanh(h)


def _ada_norm(x, nw, shift, scale):
    ms = jnp.mean(x * x, axis=-1, keepdims=True)
    y = x * lax.rsqrt(ms + NORM_EPS)
    return (y * nw) * (1.0 + scale) + shift


def _pos_tile(rt_ref, ct_ref, tile_idx, t):
    w = GRID_WIDTH
    n_rows = t // w
    half = rt_ref.shape[1]
    r0 = tile_idx * n_rows
    left = jnp.concatenate([jnp.broadcast_to(rt_ref[pl.ds(r0 + rr, 1), :], (w, half)) for rr in range(n_rows)],
                           axis=0)
    right = jnp.concatenate([ct_ref[...]] * n_rows, axis=0)
    return jnp.concatenate([left, right], axis=1)


def _split3(v):
    hi = v.astype(BF16)
    r1 = v - hi.astype(F32)
    mid = r1.astype(BF16)
    lo = (r1 - mid.astype(F32)).astype(BF16)
    return hi, mid, lo


def _dot(a, b):
    return jnp.dot(a, b, preferred_element_type=F32)


def _mod_kernel(c_ref, w_ref, b_ref, o_ref):
    s = _silu(c_ref[...])
    w = w_ref[...]
    s_hi = s.astype(BF16)
    s_lo = (s - s_hi.astype(F32)).astype(BF16)
    w_hi = w.astype(BF16)
    w_lo = (w - w_hi.astype(F32)).astype(BF16)
    o_ref[...] = _dot(s_hi, w_hi) + (_dot(s_lo, w_hi) + _dot(s_hi, w_lo)) + b_ref[...]


def _modulation(c_all, w_mod, b_mod):
    depth, d, n = w_mod.shape
    rows = c_all.shape[0]
    tn = _tile(n, 1024)
    return pl.pallas_call(
        _mod_kernel,
        grid=(depth, n // tn),
        in_specs=[
            pl.BlockSpec((rows, d), lambda i, j: (0, 0)),
            pl.BlockSpec((None, d, tn), lambda i, j: (i, 0, j)),
            pl.BlockSpec((None, 1, tn), lambda i, j: (i, 0, j)),
        ],
        out_specs=pl.BlockSpec((None, rows, tn), lambda i, j: (i, 0, j)),
        out_shape=jax.ShapeDtypeStruct((depth, rows, n), F32),
        compiler_params=_cparams(("parallel", "parallel")),
        name="modulation",
    )(c_all, w_mod, b_mod.reshape(depth, 1, n))


def _in_proj_kernel(x_ref, *rest, with_pos):
    if with_pos:
        rt_ref, ct_ref, nw_ref, sh_ref, sc_ref, w_ref, wdt_ref, b_ref, o_ref, dt_ref, h_ref = rest
    else:
        nw_ref, sh_ref, sc_ref, w_ref, wdt_ref, b_ref, o_ref, dt_ref, h_ref = rest

    @pl.when(pl.program_id(2) == 0)
    def _():
        xv = x_ref[...]
        if with_pos:
            xv = xv + _pos_tile(rt_ref, ct_ref, pl.program_id(1), xv.shape[0])
        h = _ada_norm(xv, nw_ref[...], sh_ref[...], sc_ref[...]).astype(BF16)
        h_ref[...] = h
        v = _dot(h, wdt_ref[...]) + b_ref[...]
        dt_ref[...] = jnp.maximum(v, 0.0) + jnp.log1p(jnp.exp(-jnp.abs(v)))

    o_ref[...] = _dot(h_ref[...], w_ref[...]).astype(o_ref.dtype)


def _in_proj(x, pos, nw, shift, scale, w, layer, dt_bias):
    b, l, d = x.shape
    ndt = dt_bias.shape[1]
    n = w.shape[2] - ndt
    assert n % ndt == 0
    dt_block = n // ndt
    t = _tile(l, 1024)
    with_pos = pos is not None
    tn = _tile(n, 1024)
    pos_args = list(pos) if with_pos else []
    pos_specs = [pl.BlockSpec(a.shape, lambda i, j, k: (0, 0)) for a in pos_args]
    assert not with_pos or t % GRID_WIDTH == 0
    return pl.pallas_call(
        functools.partial(_in_proj_kernel, with_pos=with_pos),
        grid=(b, l // t, n // tn),
        in_specs=[pl.BlockSpec((None, t, d), lambda i, j, k: (i, j, 0))] + pos_specs + [
            pl.BlockSpec((1, d), lambda i, j, k: (0, 0)),
            pl.BlockSpec((None, 1, d), lambda i, j, k: (i, 0, 0)),
            pl.BlockSpec((None, 1, d), lambda i, j, k: (i, 0, 0)),
            pl.BlockSpec((None, d, tn), lambda i, j, k: (layer, 0, k)),
            pl.BlockSpec((None, d, ndt), lambda i, j, k: (layer, 0, dt_block)),
            pl.BlockSpec((1, ndt), lambda i, j, k: (0, 0)),
        ],
        out_specs=[pl.BlockSpec((None, t, tn), lambda i, j, k: (i, j, k)),
                   pl.BlockSpec((None, t, ndt), lambda i, j, k: (i, j, 0))],
        out_shape=[jax.ShapeDtypeStruct((b, l, n), BF16), jax.ShapeDtypeStruct((b, l, ndt), F32)],
        scratch_shapes=[pltpu.VMEM((t, d), BF16)],
        compiler_params=_cparams(("parallel", "parallel", "arbitrary")),
        name="in_proj",
    )(x, *pos_args, nw, shift, scale, w, w, dt_bias)


def _conv_kernel(cur_ref, prev_ref, next_ref, sel_ref, w_ref, b_ref, o_ref, *, t, n_tiles, width):
    l = pl.program_id(1)
    pad = width // 2
    halo = prev_ref.shape[0]
    sb = sel_ref.shape[1]
    zero = jnp.zeros(prev_ref.shape, prev_ref.dtype)
    prev = jnp.where(l > 0, prev_ref[...], zero)
    nxt = jnp.where(l < n_tiles - 1, next_ref[...], zero)
    ext = jnp.concatenate([prev, cur_ref[...], nxt], axis=0)
    for s in range(t // sb):
        win = ext[s * sb:(s + 1) * sb + 2 * halo]
        h = b_ref[...] + w_ref[pad:pad + 1, :] * win[halo:halo + sb].astype(F32)
        for k in range(width):
            if k != pad:
                h = h + w_ref[k:k + 1, :] * _dot(sel_ref[k], win)
        o_ref[s * sb:(s + 1) * sb, :] = (h + h * jnp.tanh(h)).astype(o_ref.dtype)


def _conv_silu(zx, conv_w, conv_b, col_offset):
    b, l, _ = zx.shape
    width, c = conv_w.shape
    pad = width // 2
    halo = 2 * V7X_SUBLANES
    assert pad <= halo
    t = _tile(l, 512)
    tc = _tile(c, 2048)
    sb = _tile(t, V7X_LANES)
    n_tiles = l // t
    off = col_offset // tc
    hb = t // halo
    last_hb = l // halo - 1
    i_idx = jnp.arange(sb)[None, :, None]
    j_idx = jnp.arange(sb + 2 * halo)[None, None, :]
    k_idx = jnp.arange(width)[:, None, None]
    sel = (j_idx == i_idx + halo + k_idx - pad).astype(BF16)
    return pl.pallas_call(
        functools.partial(_conv_kernel, t=t, n_tiles=n_tiles, width=width),
        grid=(b, n_tiles, c // tc),
        in_specs=[
            pl.BlockSpec((None, t, tc), lambda i, j, k: (i, j, off + k)),
            pl.BlockSpec((None, halo, tc), lambda i, j, k: (i, jnp.maximum(j * hb - 1, 0), off + k)),
            pl.BlockSpec((None, halo, tc), lambda i, j, k: (i, jnp.minimum((j + 1) * hb, last_hb), off + k)),
            pl.BlockSpec((width, sb, sb + 2 * halo), lambda i, j, k: (0, 0, 0)),
            pl.BlockSpec((width, tc), lambda i, j, k: (0, k)),
            pl.BlockSpec((1, tc), lambda i, j, k: (0, k)),
        ],
        out_specs=pl.BlockSpec((None, t, tc), lambda i, j, k: (i, j, k)),
        out_shape=jax.ShapeDtypeStruct((b, l, c), BF16),
        compiler_params=_cparams(("parallel", "parallel", "parallel")),
        name="conv_silu",
    )(zx, zx, zx, sel, 0.5 * conv_w, 0.5 * conv_b)


def _scan_kernel(xs_ref, b_ref, c_ref, dtc_ref, dtr_ref, alr_ref, alc_ref, skd_ref, h0_ref,
                 tm_ref, tmt_ref, madd_ref, y_ref, st_ref, *, cps, heads):
    d = pl.program_id(1)
    s = pl.program_id(3)
    q = SSD_CHUNK
    p = SSD_HEAD_DIM

    @pl.when(s == 0)
    def _():
        st_ref[...] = h0_ref[...]

    lane = lax.broadcasted_iota(jnp.int32, (q, 2 * p), 1)
    lo_half = lane < p
    tm = tm_ref[...]
    tm_t = tmt_ref[...]

    dtr = dtr_ref[...]
    la_c = dtc_ref[...] * (-jnp.exp(alr_ref[...]) * LOG2_E)
    la_r = dtr * (-jnp.exp(alc_ref[...]) * LOG2_E)
    c_hi, c_mid, c_lo = _split3(la_c)
    cum_c = _dot(tm, c_hi) + _dot(tm, c_mid) + _dot(tm, c_lo)
    r_hi, r_mid, r_lo = _split3(la_r)
    cum_r = _dot(r_hi, tm_t) + _dot(r_mid, tm_t) + _dot(r_lo, tm_t)
    tot_r = jnp.sum(la_r, axis=1, keepdims=True)
    ldt_r = jnp.log2(dtr)
    f_r = jnp.exp2(tot_r - cum_r + ldt_r)
    rowb_all = cum_r - ldt_r

    st = st_ref[...]
    for i in range(cps):
        j = i + d * (cps - 1 - 2 * i)
        off = pl.multiple_of(j * q, q)
        x16 = xs_ref[pl.ds(off, q), :]
        b16 = b_ref[pl.ds(off, q), :]
        c16 = c_ref[pl.ds(off, q), :]
        bt16 = b16.astype(F32).T.astype(BF16)
        scores16 = _dot(c16, bt16).astype(BF16)
        y_state = _dot(c16, st.astype(BF16))

        y_parts, upd_parts, ec_parts = [], [], []
        for pair in range(heads // 2):
            ms, bs, ecs = [], [], []
            for e in (2 * pair, 2 * pair + 1):
                k = i * heads + e
                colb = jnp.broadcast_to(cum_c[:, k:k + 1], (q, q))
                dec = jnp.exp2((colb - rowb_all[k:k + 1, :]) + madd_ref[...])
                ms.append(scores16 * dec.astype(BF16) + skd_ref[e])
                bs.append(bt16 * jnp.broadcast_to(f_r[k:k + 1, :], (q, q)).astype(BF16))
                ecs.append(jnp.exp2(colb))
            xp = x16[:, pair * 2 * p:(pair + 1) * 2 * p]
            zero = jnp.zeros_like(xp)
            rhs = jnp.concatenate([jnp.where(lo_half, xp, zero), jnp.where(lo_half, zero, xp)], axis=0)
            y_parts.append(_dot(jnp.concatenate(ms, axis=1), rhs))
            upd_parts.append(_dot(jnp.concatenate(bs, axis=1), rhs))
            ec_parts.append(jnp.where(lo_half, ecs[0], ecs[1]))
        ec = jnp.concatenate(ec_parts, axis=1)
        y = jnp.concatenate(y_parts, axis=1) + y_state * ec
        y_ref[pl.ds(off, q), :] = y.astype(y_ref.dtype)
        etot = jnp.where(d == 0, ec[q - 1:q, :], ec[0:1, :])
        st = st * etot + jnp.concatenate(upd_parts, axis=1)
    st_ref[...] = st


def _scan_masks():
    q = SSD_CHUNK
    row = jnp.arange(q)[:, None]
    col = jnp.arange(q)[None, :]
    tri = jnp.stack([row >= col, row <= col])
    return (tri.astype(BF16), jnp.swapaxes(tri, 1, 2).astype(BF16),
            jnp.where(tri, 0.0, NEG_BIG).astype(F32))


def _scan_chunks_per_step(l):
    return min(l // SSD_CHUNK, SCAN_CHUNKS_PER_STEP)


def _dt_layouts(dts, heads):
    b, l, _ = dts.shape
    cps = _scan_chunks_per_step(l)
    steps = l // (cps * SSD_CHUNK)
    v = dts.reshape(b, steps, cps, SSD_CHUNK, 2, SSD_GROUPS, heads)
    v = jnp.stack([v[:, :, :, :, 0], v[:, :, ::-1, :, 1]], axis=1)
    col = v.transpose(0, 1, 5, 2, 4, 3, 6).reshape(b, 2 * SSD_GROUPS, steps, SSD_CHUNK, cps * heads)
    row = v.transpose(0, 1, 5, 2, 3, 6, 4).reshape(b, 2 * SSD_GROUPS, steps, cps * heads, SSD_CHUNK)
    return col, row


def _ssd_scan(xbc, dtc5, dtr5, alog, skip_diag, h0, d_inner):
    b, l, _ = xbc.shape
    g = SSD_GROUPS
    n = SSD_STATE
    gw = d_inner // g
    heads = gw // SSD_HEAD_DIM
    cps = _scan_chunks_per_step(l)
    r = cps * SSD_CHUNK
    steps = l // r
    b_off = d_inner // n
    c_off = (d_inner + g * n) // n
    alog_t = jnp.tile(alog, (1, cps))
    alog_row = alog_t[:, None, :]
    alog_col = alog_t[:, :, None]

    def cb(si, di):
        return si + di * (steps - 1 - 2 * si)

    tm, tm_t, madd = _scan_masks()
    mask_spec = pl.BlockSpec((None, SSD_CHUNK, SSD_CHUNK), lambda bi, di, gi, si: (di, 0, 0))
    return pl.pallas_call(
        functools.partial(_scan_kernel, cps=cps, heads=heads),
        grid=(b, 2, g, steps),
        in_specs=[
            pl.BlockSpec((None, r, gw), lambda bi, di, gi, si: (bi, cb(si, di), gi)),
            pl.BlockSpec((None, r, n), lambda bi, di, gi, si: (bi, cb(si, di), b_off + gi)),
            pl.BlockSpec((None, r, n), lambda bi, di, gi, si: (bi, cb(si, di), c_off + gi)),
            pl.BlockSpec((None, None, None, SSD_CHUNK, cps * heads),
                         lambda bi, di, gi, si: (bi, di * g + gi, cb(si, di), 0, 0)),
            pl.BlockSpec((None, None, None, cps * heads, SSD_CHUNK),
                         lambda bi, di, gi, si: (bi, di * g + gi, cb(si, di), 0, 0)),
            pl.BlockSpec((None, 1, cps * heads), lambda bi, di, gi, si: (di * g + gi, 0, 0)),
            pl.BlockSpec((None, cps * heads, 1), lambda bi, di, gi, si: (di * g + gi, 0, 0)),
            pl.BlockSpec((None, heads, SSD_CHUNK, SSD_CHUNK), lambda bi, di, gi, si: (di * g + gi, 0, 0, 0)),
            pl.BlockSpec((None, None, None, n, gw), lambda bi, di, gi, si: (bi, di, gi, 0, 0)),
            mask_spec, mask_spec, mask_spec,
        ],
        out_specs=[
            pl.BlockSpec((None, None, r, gw), lambda bi, di, gi, si: (di, bi, cb(si, di), gi)),
            pl.BlockSpec((None, None, None, n, gw), lambda bi, di, gi, si: (bi, di, gi, 0, 0)),
        ],
        out_shape=[jax.ShapeDtypeStruct((2, b, l, d_inner), BF16),
                   jax.ShapeDtypeStruct((b, 2, g, n, gw), F32)],
        compiler_params=_cparams(("parallel", "parallel", "parallel", "arbitrary")),
        name="ssd_scan",
    )(xbc, xbc, xbc, dtc5, dtr5, alog_row, alog_col, skip_diag, h0, tm, tm_t, madd)


def _ssd_out_kernel(yf_ref, yb_ref, z_ref, nw_ref, w_ref, x_ref, g_ref, *rest, gw, with_pos):
    if with_pos:
        rt_ref, ct_ref, o_ref, a_ref = rest
    else:
        o_ref, a_ref = rest
    for gi in range(a_ref.shape[1] // gw):
        sl = slice(gi * gw, (gi + 1) * gw)
        v = (yf_ref[:, sl].astype(F32) + yb_ref[:, sl].astype(F32)) * _silu(z_ref[:, sl].astype(F32))
        ms = jnp.mean(v * v, axis=-1, keepdims=True)
        a_ref[:, sl] = (v * lax.rsqrt(ms + SSD_NORM_EPS) * nw_ref[:, sl]).astype(BF16)
    xv = x_ref[...]
    if with_pos:
        xv = xv + _pos_tile(rt_ref, ct_ref, pl.program_id(1), xv.shape[0])
    o_ref[...] = xv + g_ref[...] * _dot(a_ref[...], w_ref[...])


def _ssd_out(y2, zx, norm_w, w_out, layer, x, gate, pos=None):
    b, l, d = x.shape
    k = w_out.shape[1]
    t = _tile(l, 256)
    with_pos = pos is not None
    pos_args = list(pos) if with_pos else []
    pos_specs = [pl.BlockSpec(a.shape, lambda i, j: (0, 0)) for a in pos_args]
    assert not with_pos or t % GRID_WIDTH == 0
    return pl.pallas_call(
        functools.partial(_ssd_out_kernel, gw=k // SSD_GROUPS, with_pos=with_pos),
        grid=(b, l // t),
        in_specs=[
            pl.BlockSpec((None, None, t, k), lambda i, j: (0, i, j, 0)),
            pl.BlockSpec((None, None, t, k), lambda i, j: (1, i, j, 0)),
            pl.BlockSpec((None, t, k), lambda i, j: (i, j, 0)),
            pl.BlockSpec((1, k), lambda i, j: (0, 0)),
            pl.BlockSpec((None, k, d), lambda i, j: (layer, 0, 0), pipeline_mode=pl.Buffered(1)),
            pl.BlockSpec((None, t, d), lambda i, j: (i, j, 0)),
            pl.BlockSpec((None, 1, d), lambda i, j: (i, 0, 0)),
        ] + pos_specs,
        out_specs=pl.BlockSpec((None, t, d), lambda i, j: (i, j, 0)),
        out_shape=jax.ShapeDtypeStruct(x.shape, F32),
        scratch_shapes=[pltpu.VMEM((t, k), BF16)],
        compiler_params=_cparams(("parallel", "parallel")),
        name="ssd_out",
    )(y2, y2, zx, norm_w, w_out, x, gate, *pos_args)


def _mm_res_kernel(a_ref, w_ref, x_ref, g_ref, o_ref):
    o_ref[...] = x_ref[...] + g_ref[...] * _dot(a_ref[...], w_ref[...])


def _matmul_residual(a, w, x, gate):
    b, l, d = x.shape
    k = w.shape[0]
    t = _tile(l, 1024)
    tn = _tile(d, 1024)
    return pl.pallas_call(
        _mm_res_kernel,
        grid=(b, l // t, d // tn),
        in_specs=[
            pl.BlockSpec((None, t, k), lambda i, j, n: (i, j, 0)),
            pl.BlockSpec((k, tn), lambda i, j, n: (0, n)),
            pl.BlockSpec((None, t, tn), lambda i, j, n: (i, j, n)),
            pl.BlockSpec((None, 1, tn), lambda i, j, n: (i, 0, n)),
        ],
        out_specs=pl.BlockSpec((None, t, tn), lambda i, j, n: (i, j, n)),
        out_shape=jax.ShapeDtypeStruct(x.shape, F32),
        compiler_params=_cparams(("parallel", "parallel", "parallel")),
        name="matmul_residual",
    )(a, w, x, gate)


def _ffn_kernel(x_ref, nw_ref, sh_ref, sc_ref, g_ref, wg_ref, wu_ref, wd_ref, *rest, final):
    if final:
        fw_ref, o_ref, h_ref = rest
    else:
        o_ref, h_ref = rest
    j = pl.program_id(2)

    @pl.when(j == 0)
    def _():
        h_ref[...] = _ada_norm(x_ref[...], nw_ref[...], sh_ref[...], sc_ref[...]).astype(BF16)
        o_ref[...] = jnp.zeros_like(o_ref)

    h = h_ref[...]
    a = _dot(h, wg_ref[...])
    u = _dot(h, wu_ref[...])
    o_ref[...] += _dot((_silu(a) * u).astype(BF16), wd_ref[...])

    @pl.when(j == pl.num_programs(2) - 1)
    def _():
        out = x_ref[...] + g_ref[...] * o_ref[...]
        if final:
            ms = jnp.mean(out * out, axis=-1, keepdims=True)
            out = out * lax.rsqrt(ms + NORM_EPS) * fw_ref[...]
        o_ref[...] = out


def _ffn(x, nw, shift, scale, gate, wg, wu, wd, layer, final_w=None):
    b, l, d = x.shape
    hdim = wg.shape[2]
    t = _tile(l, 512)
    th = _tile(hdim, 512)
    final = final_w is not None
    vec = pl.BlockSpec((None, 1, d), lambda i, j, k: (i, 0, 0))
    in_specs = [
        pl.BlockSpec((None, t, d), lambda i, j, k: (i, j, 0)),
        pl.BlockSpec((1, d), lambda i, j, k: (0, 0)),
        vec, vec, vec,
        pl.BlockSpec((None, d, th), lambda i, j, k: (layer, 0, k)),
        pl.BlockSpec((None, d, th), lambda i, j, k: (layer, 0, k)),
        pl.BlockSpec((None, th, d), lambda i, j, k: (layer, k, 0)),
    ]
    args = [x, nw, shift, scale, gate, wg, wu, wd]
    if final:
        in_specs.append(pl.BlockSpec((1, d), lambda i, j, k: (0, 0)))
        args.append(final_w)
    return pl.pallas_call(
        functools.partial(_ffn_kernel, final=final),
        grid=(b, l // t, hdim // th),
        in_specs=in_specs,
        out_specs=pl.BlockSpec((None, t, d), lambda i, j, k: (i, j, 0)),
        out_shape=jax.ShapeDtypeStruct(x.shape, F32),
        scratch_shapes=[pltpu.VMEM((t, d), BF16)],
        compiler_params=_cparams(("parallel", "parallel", "arbitrary")),
        name="ffn",
    )(*args)


def _chan_dft_kernel(x_ref, nw_ref, sh_ref, sc_ref, w_ref, o_ref, *, gw):
    h = _ada_norm(x_ref[...], nw_ref[...], sh_ref[...], sc_ref[...]).astype(BF16)
    for gi in range(h.shape[1] // gw):
        sl = slice(gi * gw, (gi + 1) * gw)
        r = _dot(h[:, sl], w_ref[...])
        o_ref[0, :, sl] = r[:, :gw].astype(BF16)
        o_ref[1, :, sl] = r[:, gw:].astype(BF16)


def _chan_dft(x, nw, shift, scale, w_cs):
    b, l, d = x.shape
    gw = w_cs.shape[0]
    t = _tile(l, 512)
    vec = pl.BlockSpec((None, 1, d), lambda i, j: (i, 0, 0))
    return pl.pallas_call(
        functools.partial(_chan_dft_kernel, gw=gw),
        grid=(b, l // t),
        in_specs=[
            pl.BlockSpec((None, t, d), lambda i, j: (i, j, 0)),
            pl.BlockSpec((1, d), lambda i, j: (0, 0)),
            vec, vec,
            pl.BlockSpec((gw, 2 * gw), lambda i, j: (0, 0)),
        ],
        out_specs=pl.BlockSpec((None, 2, t, d), lambda i, j: (i, 0, j, 0)),
        out_shape=jax.ShapeDtypeStruct((b, 2, l, d), BF16),
        compiler_params=_cparams(("parallel", "parallel")),
        name="chan_dft",
    )(x, nw, shift, scale, w_cs)


def _seq_dft_kernel(a_ref, pq_ref, o_ref, acc_ref):
    k = pl.program_id(3)

    @pl.when(k == 0)
    def _():
        acc_ref[...] = jnp.zeros_like(acc_ref)

    acc_ref[...] += _dot(a_ref[...], pq_ref[...])

    @pl.when(k == pl.num_programs(3) - 1)
    def _():
        o_ref[...] = acc_ref[...].astype(o_ref.dtype)


def _seq_dft(a_mat, pq):
    b, k2, d = pq.shape
    l = a_mat.shape[0]
    tm = _tile(l, 1024)
    tn = _tile(d, 1024)
    tk = _tile(k2, 2048)
    return pl.pallas_call(
        _seq_dft_kernel,
        grid=(b, l // tm, d // tn, k2 // tk),
        in_specs=[
            pl.BlockSpec((tm, tk), lambda i, m, n, k: (m, k)),
            pl.BlockSpec((None, tk, tn), lambda i, m, n, k: (i, k, n)),
        ],
        out_specs=pl.BlockSpec((None, tm, tn), lambda i, m, n, k: (i, m, n)),
        out_shape=jax.ShapeDtypeStruct((b, l, d), BF16),
        scratch_shapes=[pltpu.VMEM((tm, tn), F32)],
        compiler_params=_cparams(("parallel", "parallel", "parallel", "arbitrary")),
        name="seq_dft",
    )(a_mat, pq)


def _dft_tables(n, scale):
    idx = jnp.arange(n, dtype=jnp.int32)

    def cs(prod):
        ang = (prod % n).astype(F32) * (2.0 * math.pi / n)
        return jnp.cos(ang), jnp.sin(ang)

    m = GRID_WIDTH
    if n <= m or n % m:
        c, s = cs(idx[:, None] * idx[None, :])
        return c * scale, s * scale
    ca, sa = cs(idx[:, None] * (m * jnp.arange(n // m, dtype=jnp.int32))[None, :])
    cb, sb = cs(idx[:, None] * jnp.arange(m, dtype=jnp.int32)[None, :])
    ca, sa = ca[:, :, None] * scale, sa[:, :, None] * scale
    cb, sb = cb[:, None, :], sb[:, None, :]
    return (ca * cb - sa * sb).reshape(n, n), (sa * cb + ca * sb).reshape(n, n)


def _pool_kernel(cur_ref, prev_ref, next_ref, nw_ref, sh_ref, sc_ref, g_ref, w_ref, ps_ref,
                 o_ref, ext_ref, *, t, n_tiles, seq_len, gw):
    l = pl.program_id(1)
    nw, sh, sc = nw_ref[...], sh_ref[...], sc_ref[...]
    hc = _ada_norm(cur_ref[...], nw, sh, sc)
    ext_ref[0:HALO_ROWS, :] = jnp.where(l > 0, _ada_norm(prev_ref[...], nw, sh, sc), 0.0)
    ext_ref[HALO_ROWS:HALO_ROWS + t, :] = hc
    ext_ref[HALO_ROWS + t:, :] = jnp.where(l < n_tiles - 1, _ada_norm(next_ref[...], nw, sh, sc), 0.0)
    pos = l * t + lax.broadcasted_iota(jnp.int32, (t, 1), 0)
    for gi, win in enumerate(POOL_WINDOWS):
        sl = slice(gi * gw, (gi + 1) * gw)
        half = win // 2
        tot = ext_ref[pl.ds(HALO_ROWS - half, t), sl]
        for k in range(-half + 1, win - half):
            tot = tot + ext_ref[pl.ds(HALO_ROWS + k, t), sl]
        cnt = (jnp.minimum(pos + (win - half), seq_len) - jnp.maximum(pos - half, 0)).astype(F32)
        pooled = (tot / cnt - hc[:, sl]).astype(BF16)
        out = _dot(pooled, w_ref[gi]) * ps_ref[:, sl]
        o_ref[:, sl] = cur_ref[:, sl] + g_ref[:, sl] * out


def _pool_mixer(x, nw, shift, scale, gate, w_grp, pscale):
    b, l, d = x.shape
    ng, gw, _ = w_grp.shape
    assert max(POOL_WINDOWS) // 2 <= HALO_ROWS
    t = _tile(l, 512)
    n_tiles = l // t
    rb = t // HALO_ROWS
    last_rb = l // HALO_ROWS - 1
    vec = pl.BlockSpec((None, 1, d), lambda i, j: (i, 0, 0))
    return pl.pallas_call(
        functools.partial(_pool_kernel, t=t, n_tiles=n_tiles, seq_len=l, gw=gw),
        grid=(b, n_tiles),
        in_specs=[
            pl.BlockSpec((None, t, d), lambda i, j: (i, j, 0)),
            pl.BlockSpec((None, HALO_ROWS, d), lambda i, j: (i, jnp.maximum(j * rb - 1, 0), 0)),
            pl.BlockSpec((None, HALO_ROWS, d), lambda i, j: (i, jnp.minimum((j + 1) * rb, last_rb), 0)),
            pl.BlockSpec((1, d), lambda i, j: (0, 0)),
            vec, vec, vec,
            pl.BlockSpec((ng, gw, gw), lambda i, j: (0, 0, 0)),
            pl.BlockSpec((1, d), lambda i, j: (0, 0)),
        ],
        out_specs=pl.BlockSpec((None, t, d), lambda i, j: (i, j, 0)),
        out_shape=jax.ShapeDtypeStruct(x.shape, F32),
        scratch_shapes=[pltpu.VMEM((t + 2 * HALO_ROWS, d), F32)],
        compiler_params=_cparams(("parallel", "parallel")),
        name="pool_mixer",
    )(x, x, x, nw, shift, scale, gate, w_grp, pscale)


def _pos_tables(n_tokens, d):
    rows = n_tokens // GRID_WIDTH
    quarter = d // 4
    omega = 1.0 / (POS_BASE ** (jnp.arange(quarter, dtype=F32) / quarter))

    def table(n):
        v = jnp.arange(n, dtype=F32)[:, None] * omega
        return jnp.concatenate([jnp.sin(v), jnp.cos(v)], axis=-1)

    return table(rows), table(GRID_WIDTH)


def _flat(u):
    return u.reshape((1, u.shape[0] * u.shape[1]) + u.shape[2:])


def _unflat(u, b):
    return u.reshape((b, u.shape[0] * u.shape[1] // b) + u.shape[2:])


def _ssd_layer(x, ctx, mods, mods_c, nw, p, with_ctx, pos=None):
    sh1, sc1, g1 = mods[0], mods[1], mods[2]
    ch1, cs1, cg1 = mods_c[0], mods_c[1], mods_c[2]
    d_inner = p["out_proj"].shape[1]
    heads = p["dt_bias"].shape[-1]
    hpg = heads // SSD_GROUPS
    gw = d_inner // SSD_GROUPS
    dt_bias = p["dt_bias"].reshape(1, 2 * heads)
    alog = p["a_log"].reshape(2 * SSD_GROUPS, hpg)
    skip_diag = (p["d"].reshape(2 * SSD_GROUPS, hpg)[:, :, None, None]
               * jnp.eye(SSD_CHUNK, dtype=F32)[None, None]).astype(BF16)

    b = x.shape[0]

    def project(u, shift, scale, u_pos=None):
        zx, dts = _in_proj(u, u_pos, nw, shift, scale, p["in_proj"], p["layer"], dt_bias)
        xbc = _conv_silu(_unflat(zx, b), p["conv_w"], p["conv_b"], d_inner)
        dtc5, dtr5 = _dt_layouts(_unflat(dts, b), hpg)
        return zx, xbc, dtc5, dtr5

    zx_c, xbc_c, dtc_c, dtr_c = project(_flat(ctx), ch1, cs1)
    zx_l, xbc_l, dtc_l, dtr_l = project(x, sh1, sc1, pos)
    h0 = jnp.zeros((b, 2, SSD_GROUPS, SSD_STATE, gw), F32)
    y_c, st_c = _ssd_scan(xbc_c, dtc_c, dtr_c, alog, skip_diag, h0, d_inner)
    y_l, _ = _ssd_scan(xbc_l, dtc_l, dtr_l, alog, skip_diag, st_c, d_inner)
    x = _ssd_out(y_l, zx_l, p["norm_w"], p["out_proj"], p["layer"], x, g1, pos)
    if with_ctx:
        y_flat = y_c.reshape(2, 1, -1, d_inner)
        ctx = _unflat(_ssd_out(y_flat, zx_c, p["norm_w"], p["out_proj"], p["layer"], _flat(ctx), cg1), b)
    return x, ctx


def _fourier_layer(x, ctx, mods, mods_c, nw, w_out, with_ctx):
    d = x.shape[-1]
    gw = d // N_FOURIER_GROUPS
    cc, sc_ = _dft_tables(gw, gw ** -0.5)
    w_cs = jnp.concatenate([cc, sc_], axis=1).astype(BF16)

    def mix(u, shift, scale, gate):
        l = u.shape[1]
        cl, sl = _dft_tables(l, l ** -0.5)
        a_mat = jnp.concatenate([cl, -sl], axis=1).astype(BF16)
        pq = _chan_dft(u, nw, shift, scale, w_cs)
        f = _seq_dft(a_mat, pq.reshape(u.shape[0], 2 * l, d))
        if gate.shape[0] == 1:
            return _unflat(_matmul_residual(_flat(f), w_out, _flat(u), gate), u.shape[0])
        return _matmul_residual(f, w_out, u, gate)

    x = mix(x, mods[0], mods[1], mods[2])
    if with_ctx:
        b = ctx.shape[0]
        ctx = mix(ctx, jnp.broadcast_to(mods_c[0], (b, 1, d)), jnp.broadcast_to(mods_c[1], (b, 1, d)), mods_c[2])
    return x, ctx


def kernel(x, c, ctx, c_ctx, w_mod, b_mod, norm_w, ffn_w_gate, ffn_w_up, ffn_w_down, ssd_in_proj, ssd_conv_w,
           ssd_conv_b, ssd_dt_bias, ssd_a_log, ssd_d, ssd_norm_w, ssd_out_proj, fourier_w_out, pool_w, pool_scale,
           final_norm_w):
    b, l, d = x.shape
    depth = w_mod.shape[0]
    n_ctx_rows = V7X_SUBLANES
    c_all = jnp.concatenate([c, jnp.broadcast_to(c_ctx[None, :], (n_ctx_rows, d))], axis=0)
    mod_all = _modulation(c_all, w_mod, b_mod)

    wg16, wu16, wd16 = ffn_w_gate.astype(BF16), ffn_w_up.astype(BF16), ffn_w_down.astype(BF16)
    in_proj16, out_proj16 = ssd_in_proj.astype(BF16), ssd_out_proj.astype(BF16)

    pos = _pos_tables(l, d)
    for i in range(depth):
        kind, j = i % N_MIXERS, i // N_MIXERS
        last = i == depth - 1
        mods = [mod_all[i, :b, None, k * d:(k + 1) * d] for k in range(6)]
        mods_c = [mod_all[i, b:b + 1, None, k * d:(k + 1) * d] for k in range(6)]
        nw1 = norm_w[i, 0][None, :]
        nw2 = norm_w[i, 1][None, :]
        if kind == 0:
            p = dict(in_proj=in_proj16, out_proj=out_proj16, layer=j, conv_w=ssd_conv_w[j],
                     conv_b=ssd_conv_b[j][None, :], dt_bias=ssd_dt_bias[j], a_log=ssd_a_log[j], d=ssd_d[j],
                     norm_w=ssd_norm_w[j][None, :])
            x, ctx = _ssd_layer(x, ctx, mods, mods_c, nw1, p, not last, pos if i == 0 else None)
        elif kind == 1:
            x, ctx = _fourier_layer(x, ctx, mods, mods_c, nw1, fourier_w_out[j].astype(BF16), not last)
        else:
            pw = pool_w[j].astype(BF16)
            ps = pool_scale[j][None, :]
            x = _pool_mixer(x, nw1, mods[0], mods[1], mods[2], pw, ps)
            if not last:
                ctx = _pool_mixer(ctx, nw1, *[jnp.broadcast_to(m, (b, 1, d)) for m in mods_c[:3]], pw, ps)
        fw = final_norm_w[None, :] if last else None
        x = _ffn(x, nw2, mods[3], mods[4], mods[5], wg16, wu16, wd16, i, fw)
        if not last:
            ctx = _unflat(_ffn(_flat(ctx), nw2, mods_c[3], mods_c[4], mods_c[5], wg16, wu16, wd16, i), b)
    return x
```

```python
import functools
import math

import jax
import jax.numpy as jnp
from jax import lax
from jax.experimental import pallas as pl
from jax.experimental.pallas import tpu as pltpu

F32 = jnp.float32
BF16 = jnp.bfloat16

GRID_WIDTH = 64
POS_BASE = 10000.0
NORM_EPS = 1e-6
SSD_NORM_EPS = 1e-5
SSD_HEAD_DIM = 64
SSD_GROUPS = 8
SSD_STATE = 128
SSD_CHUNK = 128
N_FOURIER_GROUPS = 4
POOL_WINDOWS = (2, 4, 8, 16)
N_MIXERS = 3
SCAN_CHUNKS_PER_STEP = 32

V7X_LANES = 128
V7X_SUBLANES = 8
V7X_VMEM_BYTES = 64 * 1024 * 1024
V7X_VMEM_LIMIT_BYTES = V7X_VMEM_BYTES - 8 * 1024 * 1024
HALO_ROWS = V7X_SUBLANES
NEG_BIG = -1e30
LOG2_E = 1.4426950408889634


def _cparams(semantics):
    return pltpu.CompilerParams(dimension_semantics=semantics,
                                vmem_limit_bytes=V7X_VMEM_LIMIT_BYTES)


def _tile(n, pref):
    t = min(n, pref)
    assert n % t == 0, (n, t)
    return t


def _silu(v):
    h = 0.5 * v
    return h + h * jnp.tanh(h)


def _ada_norm(x, nw, shift, scale):
    ms = jnp.mean(x * x, axis=-1, keepdims=True)
    y = x * lax.rsqrt(ms + NORM_EPS)
    return (y * nw) * (1.0 + scale) + shift


def _pos_tile(rt_ref, ct_ref, tile_idx, t):
    w = GRID_WIDTH
    n_rows = t // w
    half = rt_ref.shape[1]
    r0 = tile_idx * n_rows
    left = jnp.concatenate([jnp.broadcast_to(rt_ref[pl.ds(r0 + rr, 1), :], (w, half)) for rr in range(n_rows)],
                           axis=0)
    right = jnp.concatenate([ct_ref[...]] * n_rows, axis=0)
    return jnp.concatenate([left, right], axis=1)


def _split3(v):
    hi = v.astype(BF16)
    r1 = v - hi.astype(F32)
    mid = r1.astype(BF16)
    lo = (r1 - mid.astype(F32)).astype(BF16)
    return hi, mid, lo


def _dot(a, b):
    return jnp.dot(a, b, preferred_element_type=F32)


def _mod_kernel(c_ref, w_ref, b_ref, o_ref):
    s = _silu(c_ref[...])
    w = w_ref[...]
    s_hi = s.astype(BF16)
    s_lo = (s - s_hi.astype(F32)).astype(BF16)
    w_hi = w.astype(BF16)
    w_lo = (w - w_hi.astype(F32)).astype(BF16)
    o_ref[...] = _dot(s_hi, w_hi) + (_dot(s_lo, w_hi) + _dot(s_hi, w_lo)) + b_ref[...]


def _modulation(c_all, w_mod, b_mod):
    depth, d, n = w_mod.shape
    rows = c_all.shape[0]
    tn = _tile(n, 1024)
    return pl.pallas_call(
        _mod_kernel,
        grid=(depth, n // tn),
        in_specs=[
            pl.BlockSpec((rows, d), lambda i, j: (0, 0)),
            pl.BlockSpec((None, d, tn), lambda i, j: (i, 0, j)),
            pl.BlockSpec((None, 1, tn), lambda i, j: (i, 0, j)),
        ],
        out_specs=pl.BlockSpec((None, rows, tn), lambda i, j: (i, 0, j)),
        out_shape=jax.ShapeDtypeStruct((depth, rows, n), F32),
        compiler_params=_cparams(("parallel", "parallel")),
        name="modulation",
    )(c_all, w_mod, b_mod.reshape(depth, 1, n))


def _in_proj_kernel(x_ref, *rest, with_pos):
    if with_pos:
        rt_ref, ct_ref, nw_ref, sh_ref, sc_ref, w_ref, wdt_ref, b_ref, o_ref, dt_ref, h_ref = rest
    else:
        nw_ref, sh_ref, sc_ref, w_ref, wdt_ref, b_ref, o_ref, dt_ref, h_ref = rest

    @pl.when(pl.program_id(2) == 0)
    def _():
        xv = x_ref[...]
        if with_pos:
            xv = xv + _pos_tile(rt_ref, ct_ref, pl.program_id(1), xv.shape[0])
        h = _ada_norm(xv, nw_ref[...], sh_ref[...], sc_ref[...]).astype(BF16)
        h_ref[...] = h
        v = _dot(h, wdt_ref[...]) + b_ref[...]
        dt_ref[...] = jnp.maximum(v, 0.0) + jnp.log1p(jnp.exp(-jnp.abs(v)))

    o_ref[...] = _dot(h_ref[...], w_ref[...]).astype(o_ref.dtype)


def _in_proj(x, pos, nw, shift, scale, w, layer, dt_bias):
    b, l, d = x.shape
    ndt = dt_bias.shape[1]
    n = w.shape[2] - ndt
    assert n % ndt == 0
    dt_block = n // ndt
    t = _tile(l, 1024)
    with_pos = pos is not None
    tn = _tile(n, 1024 if with_pos else 2048)
    pos_args = list(pos) if with_pos else []
    pos_specs = [pl.BlockSpec(a.shape, lambda i, j, k: (0, 0)) for a in pos_args]
    assert not with_pos or t % GRID_WIDTH == 0
    return pl.pallas_call(
        functools.partial(_in_proj_kernel, with_pos=with_pos),
        grid=(b, l // t, n // tn),
        in_specs=[pl.BlockSpec((None, t, d), lambda i, j, k: (i, j, 0))] + pos_specs + [
            pl.BlockSpec((1, d), lambda i, j, k: (0, 0)),
            pl.BlockSpec((None, 1, d), lambda i, j, k: (i, 0, 0)),
            pl.BlockSpec((None, 1, d), lambda i, j, k: (i, 0, 0)),
            pl.BlockSpec((None, d, tn), lambda i, j, k: (layer, 0, k)),
            pl.BlockSpec((None, d, ndt), lambda i, j, k: (layer, 0, dt_block)),
            pl.BlockSpec((1, ndt), lambda i, j, k: (0, 0)),
        ],
        out_specs=[pl.BlockSpec((None, t, tn), lambda i, j, k: (i, j, k)),
                   pl.BlockSpec((None, t, ndt), lambda i, j, k: (i, j, 0))],
        out_shape=[jax.ShapeDtypeStruct((b, l, n), BF16), jax.ShapeDtypeStruct((b, l, ndt), F32)],
        scratch_shapes=[pltpu.VMEM((t, d), BF16)],
        compiler_params=_cparams(("parallel", "parallel", "arbitrary")),
        name="in_proj",
    )(x, *pos_args, nw, shift, scale, w, w, dt_bias)


def _conv_kernel(cur_ref, prev_ref, next_ref, sel_ref, w_ref, b_ref, o_ref, *, t, n_tiles, width):
    l = pl.program_id(1)
    pad = width // 2
    halo = prev_ref.shape[0]
    sb = sel_ref.shape[1]
    zero = jnp.zeros(prev_ref.shape, prev_ref.dtype)
    prev = jnp.where(l > 0, prev_ref[...], zero)
    nxt = jnp.where(l < n_tiles - 1, next_ref[...], zero)
    ext = jnp.concatenate([prev, cur_ref[...], nxt], axis=0)
    for s in range(t // sb):
        win = ext[s * sb:(s + 1) * sb + 2 * halo]
        h = b_ref[...] + w_ref[pad:pad + 1, :] * win[halo:halo + sb].astype(F32)
        for k in range(width):
            if k != pad:
                h = h + w_ref[k:k + 1, :] * _dot(sel_ref[k], win)
        o_ref[s * sb:(s + 1) * sb, :] = (h + h * jnp.tanh(h)).astype(o_ref.dtype)


def _conv_silu(zx, conv_w, conv_b, col_offset):
    b, l, _ = zx.shape
    width, c = conv_w.shape
    pad = width // 2
    halo = 2 * V7X_SUBLANES
    assert pad <= halo
    t = _tile(l, 512)
    tc = _tile(c, 2048)
    sb = _tile(t, V7X_LANES)
    n_tiles = l // t
    off = col_offset // tc
    hb = t // halo
    last_hb = l // halo - 1
    i_idx = jnp.arange(sb)[None, :, None]
    j_idx = jnp.arange(sb + 2 * halo)[None, None, :]
    k_idx = jnp.arange(width)[:, None, None]
    sel = (j_idx == i_idx + halo + k_idx - pad).astype(BF16)
    return pl.pallas_call(
        functools.partial(_conv_kernel, t=t, n_tiles=n_tiles, width=width),
        grid=(b, n_tiles, c // tc),
        in_specs=[
            pl.BlockSpec((None, t, tc), lambda i, j, k: (i, j, off + k)),
            pl.BlockSpec((None, halo, tc), lambda i, j, k: (i, jnp.maximum(j * hb - 1, 0), off + k)),
            pl.BlockSpec((None, halo, tc), lambda i, j, k: (i, jnp.minimum((j + 1) * hb, last_hb), off + k)),
            pl.BlockSpec((width, sb, sb + 2 * halo), lambda i, j, k: (0, 0, 0)),
            pl.BlockSpec((width, tc), lambda i, j, k: (0, k)),
            pl.BlockSpec((1, tc), lambda i, j, k: (0, k)),
        ],
        out_specs=pl.BlockSpec((None, t, tc), lambda i, j, k: (i, j, k)),
        out_shape=jax.ShapeDtypeStruct((b, l, c), BF16),
        compiler_params=_cparams(("parallel", "parallel", "parallel")),
        name="conv_silu",
    )(zx, zx, zx, sel, 0.5 * conv_w, 0.5 * conv_b)


def _scan_kernel(xs_ref, b_ref, c_ref, dtc_ref, dtr_ref, alr_ref, alc_ref, skd_ref, h0_ref,
                 tm_ref, tmt_ref, madd_ref, y_ref, st_ref, *, cps, heads):
    d = pl.program_id(1)
    s = pl.program_id(3)
    q = SSD_CHUNK
    p = SSD_HEAD_DIM

    @pl.when(s == 0)
    def _():
        st_ref[...] = h0_ref[...]

    lane = lax.broadcasted_iota(jnp.int32, (q, 2 * p), 1)
    lo_half = lane < p
    tm = tm_ref[...]
    tm_t = tmt_ref[...]

    dtr = dtr_ref[...]
    la_c = dtc_ref[...] * (-jnp.exp(alr_ref[...]) * LOG2_E)
    la_r = dtr * (-jnp.exp(alc_ref[...]) * LOG2_E)
    c_hi, c_mid, c_lo = _split3(la_c)
    cum_c = _dot(tm, c_hi) + _dot(tm, c_mid) + _dot(tm, c_lo)
    r_hi, r_mid, r_lo = _split3(la_r)
    cum_r = _dot(r_hi, tm_t) + _dot(r_mid, tm_t) + _dot(r_lo, tm_t)
    tot_r = jnp.sum(la_r, axis=1, keepdims=True)
    ldt_r = jnp.log2(dtr)
    f_r = jnp.exp2(tot_r - cum_r + ldt_r)
    rowb_all = cum_r - ldt_r

    st = st_ref[...]
    for i in range(cps):
        j = i + d * (cps - 1 - 2 * i)
        off = pl.multiple_of(j * q, q)
        x16 = xs_ref[pl.ds(off, q), :]
        b16 = b_ref[pl.ds(off, q), :]
        c16 = c_ref[pl.ds(off, q), :]
        bt16 = b16.astype(F32).T.astype(BF16)
        scores16 = _dot(c16, bt16).astype(BF16)
        y_state = _dot(c16, st.astype(BF16))

        y_parts, upd_parts, ec_parts = [], [], []
        for pair in range(heads // 2):
            ms, bs, ecs = [], [], []
            for e in (2 * pair, 2 * pair + 1):
                k = i * heads + e
                colb = jnp.broadcast_to(cum_c[:, k:k + 1], (q, q))
                dec = jnp.exp2((colb - rowb_all[k:k + 1, :]) + madd_ref[...])
                ms.append(scores16 * dec.astype(BF16) + skd_ref[e])
                bs.append(bt16 * jnp.broadcast_to(f_r[k:k + 1, :], (q, q)).astype(BF16))
                ecs.append(jnp.exp2(colb))
            xp = x16[:, pair * 2 * p:(pair + 1) * 2 * p]
            zero = jnp.zeros_like(xp)
            rhs = jnp.concatenate([jnp.where(lo_half, xp, zero), jnp.where(lo_half, zero, xp)], axis=0)
            y_parts.append(_dot(jnp.concatenate(ms, axis=1), rhs))
            upd_parts.append(_dot(jnp.concatenate(bs, axis=1), rhs))
            ec_parts.append(jnp.where(lo_half, ecs[0], ecs[1]))
        ec = jnp.concatenate(ec_parts, axis=1)
        y = jnp.concatenate(y_parts, axis=1) + y_state * ec
        y_ref[pl.ds(off, q), :] = y.astype(y_ref.dtype)
        etot = jnp.where(d == 0, ec[q - 1:q, :], ec[0:1, :])
        st = st * etot + jnp.concatenate(upd_parts, axis=1)
    st_ref[...] = st


def _scan_masks():
    q = SSD_CHUNK
    row = jnp.arange(q)[:, None]
    col = jnp.arange(q)[None, :]
    tri = jnp.stack([row >= col, row <= col])
    return (tri.astype(BF16), jnp.swapaxes(tri, 1, 2).astype(BF16),
            jnp.where(tri, 0.0, NEG_BIG).astype(F32))


def _scan_chunks_per_step(l):
    return min(l // SSD_CHUNK, SCAN_CHUNKS_PER_STEP)


def _dt_layouts(dts, heads):
    b, l, _ = dts.shape
    cps = _scan_chunks_per_step(l)
    steps = l // (cps * SSD_CHUNK)
    v = dts.reshape(b, steps, cps, SSD_CHUNK, 2, SSD_GROUPS, heads)
    v = jnp.stack([v[:, :, :, :, 0], v[:, :, ::-1, :, 1]], axis=1)
    col = v.transpose(0, 1, 5, 2, 4, 3, 6).reshape(b, 2 * SSD_GROUPS, steps, SSD_CHUNK, cps * heads)
    row = v.transpose(0, 1, 5, 2, 3, 6, 4).reshape(b, 2 * SSD_GROUPS, steps, cps * heads, SSD_CHUNK)
    return col, row


def _ssd_scan(xbc, dtc5, dtr5, alog, skip_diag, h0, d_inner):
    b, l, _ = xbc.shape
    g = SSD_GROUPS
    n = SSD_STATE
    gw = d_inner // g
    heads = gw // SSD_HEAD_DIM
    cps = _scan_chunks_per_step(l)
    r = cps * SSD_CHUNK
    steps = l // r
    b_off = d_inner // n
    c_off = (d_inner + g * n) // n
    alog_t = jnp.tile(alog, (1, cps))
    alog_row = alog_t[:, None, :]
    alog_col = alog_t[:, :, None]

    def cb(si, di):
        return si + di * (steps - 1 - 2 * si)

    tm, tm_t, madd = _scan_masks()
    mask_spec = pl.BlockSpec((None, SSD_CHUNK, SSD_CHUNK), lambda bi, di, gi, si: (di, 0, 0))
    return pl.pallas_call(
        functools.partial(_scan_kernel, cps=cps, heads=heads),
        grid=(b, 2, g, steps),
        in_specs=[
            pl.BlockSpec((None, r, gw), lambda bi, di, gi, si: (bi, cb(si, di), gi)),
            pl.BlockSpec((None, r, n), lambda bi, di, gi, si: (bi, cb(si, di), b_off + gi)),
            pl.BlockSpec((None, r, n), lambda bi, di, gi, si: (bi, cb(si, di), c_off + gi)),
            pl.BlockSpec((None, None, None, SSD_CHUNK, cps * heads),
                         lambda bi, di, gi, si: (bi, di * g + gi, cb(si, di), 0, 0)),
            pl.BlockSpec((None, None, None, cps * heads, SSD_CHUNK),
                         lambda bi, di, gi, si: (bi, di * g + gi, cb(si, di), 0, 0)),
            pl.BlockSpec((None, 1, cps * heads), lambda bi, di, gi, si: (di * g + gi, 0, 0)),
            pl.BlockSpec((None, cps * heads, 1), lambda bi, di, gi, si: (di * g + gi, 0, 0)),
            pl.BlockSpec((None, heads, SSD_CHUNK, SSD_CHUNK), lambda bi, di, gi, si: (di * g + gi, 0, 0, 0)),
            pl.BlockSpec((None, None, None, n, gw), lambda bi, di, gi, si: (bi, di, gi, 0, 0)),
            mask_spec, mask_spec, mask_spec,
        ],
        out_specs=[
            pl.BlockSpec((None, None, r, gw), lambda bi, di, gi, si: (di, bi, cb(si, di), gi)),
            pl.BlockSpec((None, None, None, n, gw), lambda bi, di, gi, si: (bi, di, gi, 0, 0)),
        ],
        out_shape=[jax.ShapeDtypeStruct((2, b, l, d_inner), BF16),
                   jax.ShapeDtypeStruct((b, 2, g, n, gw), F32)],
        compiler_params=_cparams(("parallel", "parallel", "parallel", "arbitrary")),
        name="ssd_scan",
    )(xbc, xbc, xbc, dtc5, dtr5, alog_row, alog_col, skip_diag, h0, tm, tm_t, madd)


def _ssd_out_kernel(yf_ref, yb_ref, z_ref, nw_ref, w_ref, x_ref, g_ref, *rest, gw, with_pos):
    if with_pos:
        rt_ref, ct_ref, o_ref, a_ref = rest
    else:
        o_ref, a_ref = rest
    for gi in range(a_ref.shape[1] // gw):
        sl = slice(gi * gw, (gi + 1) * gw)
        v = (yf_ref[:, sl].astype(F32) + yb_ref[:, sl].astype(F32)) * _silu(z_ref[:, sl].astype(F32))
        ms = jnp.mean(v * v, axis=-1, keepdims=True)
        a_ref[:, sl] = (v * lax.rsqrt(ms + SSD_NORM_EPS) * nw_ref[:, sl]).astype(BF16)
    xv = x_ref[...]
    if with_pos:
        xv = xv + _pos_tile(rt_ref, ct_ref, pl.program_id(1), xv.shape[0])
    o_ref[...] = xv + g_ref[...] * _dot(a_ref[...], w_ref[...])


def _ssd_out(y2, zx, norm_w, w_out, layer, x, gate, pos=None):
    b, l, d = x.shape
    k = w_out.shape[1]
    t = _tile(l, 256)
    with_pos = pos is not None
    pos_args = list(pos) if with_pos else []
    pos_specs = [pl.BlockSpec(a.shape, lambda i, j: (0, 0)) for a in pos_args]
    assert not with_pos or t % GRID_WIDTH == 0
    return pl.pallas_call(
        functools.partial(_ssd_out_kernel, gw=k // SSD_GROUPS, with_pos=with_pos),
        grid=(b, l // t),
        in_specs=[
            pl.BlockSpec((None, None, t, k), lambda i, j: (0, i, j, 0)),
            pl.BlockSpec((None, None, t, k), lambda i, j: (1, i, j, 0)),
            pl.BlockSpec((None, t, k), lambda i, j: (i, j, 0)),
            pl.BlockSpec((1, k), lambda i, j: (0, 0)),
            pl.BlockSpec((None, k, d), lambda i, j: (layer, 0, 0), pipeline_mode=pl.Buffered(1)),
            pl.BlockSpec((None, t, d), lambda i, j: (i, j, 0)),
            pl.BlockSpec((None, 1, d), lambda i, j: (i, 0, 0)),
        ] + pos_specs,
        out_specs=pl.BlockSpec((None, t, d), lambda i, j: (i, j, 0)),
        out_shape=jax.ShapeDtypeStruct(x.shape, F32),
        scratch_shapes=[pltpu.VMEM((t, k), BF16)],
        compiler_params=_cparams(("parallel", "parallel")),
        name="ssd_out",
    )(y2, y2, zx, norm_w, w_out, x, gate, *pos_args)


def _mm_res_kernel(a_ref, w_ref, x_ref, g_ref, o_ref):
    o_ref[...] = x_ref[...] + g_ref[...] * _dot(a_ref[...], w_ref[...])


def _matmul_residual(a, w, x, gate):
    b, l, d = x.shape
    k = w.shape[0]
    t = _tile(l, 1024)
    tn = _tile(d, 1024)
    return pl.pallas_call(
        _mm_res_kernel,
        grid=(b, l // t, d // tn),
        in_specs=[
            pl.BlockSpec((None, t, k), lambda i, j, n: (i, j, 0)),
            pl.BlockSpec((k, tn), lambda i, j, n: (0, n)),
            pl.BlockSpec((None, t, tn), lambda i, j, n: (i, j, n)),
            pl.BlockSpec((None, 1, tn), lambda i, j, n: (i, 0, n)),
        ],
        out_specs=pl.BlockSpec((None, t, tn), lambda i, j, n: (i, j, n)),
        out_shape=jax.ShapeDtypeStruct(x.shape, F32),
        compiler_params=_cparams(("parallel", "parallel", "parallel")),
        name="matmul_residual",
    )(a, w, x, gate)


def _ffn_kernel(x_ref, nw_ref, sh_ref, sc_ref, g_ref, wg_ref, wu_ref, wd_ref, *rest, final):
    if final:
        fw_ref, o_ref, h_ref = rest
    else:
        o_ref, h_ref = rest
    j = pl.program_id(2)

    @pl.when(j == 0)
    def _():
        h_ref[...] = _ada_norm(x_ref[...], nw_ref[...], sh_ref[...], sc_ref[...]).astype(BF16)
        o_ref[...] = jnp.zeros_like(o_ref)

    h = h_ref[...]
    a = _dot(h, wg_ref[...])
    u = _dot(h, wu_ref[...])
    o_ref[...] += _dot((_silu(a) * u).astype(BF16), wd_ref[...])

    @pl.when(j == pl.num_programs(2) - 1)
    def _():
        out = x_ref[...] + g_ref[...] * o_ref[...]
        if final:
            ms = jnp.mean(out * out, axis=-1, keepdims=True)
            out = out * lax.rsqrt(ms + NORM_EPS) * fw_ref[...]
        o_ref[...] = out


def _ffn(x, nw, shift, scale, gate, wg, wu, wd, layer, final_w=None):
    b, l, d = x.shape
    hdim = wg.shape[2]
    t = _tile(l, 512)
    th = _tile(hdim, 512)
    final = final_w is not None
    vec = pl.BlockSpec((None, 1, d), lambda i, j, k: (i, 0, 0))
    in_specs = [
        pl.BlockSpec((None, t, d), lambda i, j, k: (i, j, 0)),
        pl.BlockSpec((1, d), lambda i, j, k: (0, 0)),
        vec, vec, vec,
        pl.BlockSpec((None, d, th), lambda i, j, k: (layer, 0, k)),
        pl.BlockSpec((None, d, th), lambda i, j, k: (layer, 0, k)),
        pl.BlockSpec((None, th, d), lambda i, j, k: (layer, k, 0)),
    ]
    args = [x, nw, shift, scale, gate, wg, wu, wd]
    if final:
        in_specs.append(pl.BlockSpec((1, d), lambda i, j, k: (0, 0)))
        args.append(final_w)
    return pl.pallas_call(
        functools.partial(_ffn_kernel, final=final),
        grid=(b, l // t, hdim // th),
        in_specs=in_specs,
        out_specs=pl.BlockSpec((None, t, d), lambda i, j, k: (i, j, 0)),
        out_shape=jax.ShapeDtypeStruct(x.shape, F32),
        scratch_shapes=[pltpu.VMEM((t, d), BF16)],
        compiler_params=_cparams(("parallel", "parallel", "arbitrary")),
        name="ffn",
    )(*args)


def _chan_dft_kernel(x_ref, nw_ref, sh_ref, sc_ref, w_ref, o_ref, *, gw):
    h = _ada_norm(x_ref[...], nw_ref[...], sh_ref[...], sc_ref[...]).astype(BF16)
    for gi in range(h.shape[1] // gw):
        sl = slice(gi * gw, (gi + 1) * gw)
        r = _dot(h[:, sl], w_ref[...])
        o_ref[0, :, sl] = r[:, :gw].astype(BF16)
        o_ref[1, :, sl] = r[:, gw:].astype(BF16)


def _chan_dft(x, nw, shift, scale, w_cs):
    b, l, d = x.shape
    gw = w_cs.shape[0]
    t = _tile(l, 512)
    vec = pl.BlockSpec((None, 1, d), lambda i, j: (i, 0, 0))
    return pl.pallas_call(
        functools.partial(_chan_dft_kernel, gw=gw),
        grid=(b, l // t),
        in_specs=[
            pl.BlockSpec((None, t, d), lambda i, j: (i, j, 0)),
            pl.BlockSpec((1, d), lambda i, j: (0, 0)),
            vec, vec,
            pl.BlockSpec((gw, 2 * gw), lambda i, j: (0, 0)),
        ],
        out_specs=pl.BlockSpec((None, 2, t, d), lambda i, j: (i, 0, j, 0)),
        out_shape=jax.ShapeDtypeStruct((b, 2, l, d), BF16),
        compiler_params=_cparams(("parallel", "parallel")),
        name="chan_dft",
    )(x, nw, shift, scale, w_cs)


def _seq_dft_kernel(a_ref, pq_ref, o_ref, acc_ref):
    k = pl.program_id(3)

    @pl.when(k == 0)
    def _():
        acc_ref[...] = jnp.zeros_like(acc_ref)

    acc_ref[...] += _dot(a_ref[...], pq_ref[...])

    @pl.when(k == pl.num_programs(3) - 1)
    def _():
        o_ref[...] = acc_ref[...].astype(o_ref.dtype)


def _seq_dft(a_mat, pq):
    b, k2, d = pq.shape
    l = a_mat.shape[0]
    tm = _tile(l, 1024)
    tn = _tile(d, 1024)
    tk = _tile(k2, 2048)
    return pl.pallas_call(
        _seq_dft_kernel,
        grid=(b, l // tm, d // tn, k2 // tk),
        in_specs=[
            pl.BlockSpec((tm, tk), lambda i, m, n, k: (m, k)),
            pl.BlockSpec((None, tk, tn), lambda i, m, n, k: (i, k, n)),
        ],
        out_specs=pl.BlockSpec((None, tm, tn), lambda i, m, n, k: (i, m, n)),
        out_shape=jax.ShapeDtypeStruct((b, l, d), BF16),
        scratch_shapes=[pltpu.VMEM((tm, tn), F32)],
        compiler_params=_cparams(("parallel", "parallel", "parallel", "arbitrary")),
        name="seq_dft",
    )(a_mat, pq)


def _dft_tables(n, scale):
    idx = jnp.arange(n, dtype=jnp.int32)

    def cs(prod):
        ang = (prod % n).astype(F32) * (2.0 * math.pi / n)
        return jnp.cos(ang), jnp.sin(ang)

    m = GRID_WIDTH
    if n <= m or n % m:
        c, s = cs(idx[:, None] * idx[None, :])
        return c * scale, s * scale
    ca, sa = cs(idx[:, None] * (m * jnp.arange(n // m, dtype=jnp.int32))[None, :])
    cb, sb = cs(idx[:, None] * jnp.arange(m, dtype=jnp.int32)[None, :])
    ca, sa = ca[:, :, None] * scale, sa[:, :, None] * scale
    cb, sb = cb[:, None, :], sb[:, None, :]
    return (ca * cb - sa * sb).reshape(n, n), (sa * cb + ca * sb).reshape(n, n)


def _pool_kernel(cur_ref, prev_ref, next_ref, nw_ref, sh_ref, sc_ref, g_ref, w_ref, ps_ref,
                 o_ref, ext_ref, *, t, n_tiles, seq_len, gw):
    l = pl.program_id(1)
    nw, sh, sc = nw_ref[...], sh_ref[...], sc_ref[...]
    hc = _ada_norm(cur_ref[...], nw, sh, sc)
    ext_ref[0:HALO_ROWS, :] = jnp.where(l > 0, _ada_norm(prev_ref[...], nw, sh, sc), 0.0)
    ext_ref[HALO_ROWS:HALO_ROWS + t, :] = hc
    ext_ref[HALO_ROWS + t:, :] = jnp.where(l < n_tiles - 1, _ada_norm(next_ref[...], nw, sh, sc), 0.0)
    pos = l * t + lax.broadcasted_iota(jnp.int32, (t, 1), 0)
    for gi, win in enumerate(POOL_WINDOWS):
        sl = slice(gi * gw, (gi + 1) * gw)
        half = win // 2
        tot = ext_ref[pl.ds(HALO_ROWS - half, t), sl]
        for k in range(-half + 1, win - half):
            tot = tot + ext_ref[pl.ds(HALO_ROWS + k, t), sl]
        cnt = (jnp.minimum(pos + (win - half), seq_len) - jnp.maximum(pos - half, 0)).astype(F32)
        pooled = (tot / cnt - hc[:, sl]).astype(BF16)
        out = _dot(pooled, w_ref[gi]) * ps_ref[:, sl]
        o_ref[:, sl] = cur_ref[:, sl] + g_ref[:, sl] * out


def _pool_mixer(x, nw, shift, scale, gate, w_grp, pscale):
    b, l, d = x.shape
    ng, gw, _ = w_grp.shape
    assert max(POOL_WINDOWS) // 2 <= HALO_ROWS
    t = _tile(l, 512)
    n_tiles = l // t
    rb = t // HALO_ROWS
    last_rb = l // HALO_ROWS - 1
    vec = pl.BlockSpec((None, 1, d), lambda i, j: (i, 0, 0))
    return pl.pallas_call(
        functools.partial(_pool_kernel, t=t, n_tiles=n_tiles, seq_len=l, gw=gw),
        grid=(b, n_tiles),
        in_specs=[
            pl.BlockSpec((None, t, d), lambda i, j: (i, j, 0)),
            pl.BlockSpec((None, HALO_ROWS, d), lambda i, j: (i, jnp.maximum(j * rb - 1, 0), 0)),
            pl.BlockSpec((None, HALO_ROWS, d), lambda i, j: (i, jnp.minimum((j + 1) * rb, last_rb), 0)),
            pl.BlockSpec((1, d), lambda i, j: (0, 0)),
            vec, vec, vec,
            pl.BlockSpec((ng, gw, gw), lambda i, j: (0, 0, 0)),
            pl.BlockSpec((1, d), lambda i, j: (0, 0)),
        ],
        out_specs=pl.BlockSpec((None, t, d), lambda i, j: (i, j, 0)),
        out_shape=jax.ShapeDtypeStruct(x.shape, F32),
        scratch_shapes=[pltpu.VMEM((t + 2 * HALO_ROWS, d), F32)],
        compiler_params=_cparams(("parallel", "parallel")),
        name="pool_mixer",
    )(x, x, x, nw, shift, scale, gate, w_grp, pscale)


def _pos_tables(n_tokens, d):
    rows = n_tokens // GRID_WIDTH
    quarter = d // 4
    omega = 1.0 / (POS_BASE ** (jnp.arange(quarter, dtype=F32) / quarter))

    def table(n):
        v = jnp.arange(n, dtype=F32)[:, None] * omega
        return jnp.concatenate([jnp.sin(v), jnp.cos(v)], axis=-1)

    return table(rows), table(GRID_WIDTH)


def _flat(u):
    return u.reshape((1, u.shape[0] * u.shape[1]) + u.shape[2:])


def _unflat(u, b):
    return u.reshape((b, u.shape[0] * u.shape[1] // b) + u.shape[2:])


def _ssd_layer(x, ctx, mods, mods_c, nw, p, with_ctx, pos=None):
    sh1, sc1, g1 = mods[0], mods[1], mods[2]
    ch1, cs1, cg1 = mods_c[0], mods_c[1], mods_c[2]
    d_inner = p["out_proj"].shape[1]
    heads = p["dt_bias"].shape[-1]
    hpg = heads // SSD_GROUPS
    gw = d_inner // SSD_GROUPS
    dt_bias = p["dt_bias"].reshape(1, 2 * heads)
    alog = p["a_log"].reshape(2 * SSD_GROUPS, hpg)
    skip_diag = (p["d"].reshape(2 * SSD_GROUPS, hpg)[:, :, None, None]
               * jnp.eye(SSD_CHUNK, dtype=F32)[None, None]).astype(BF16)

    b = x.shape[0]

    def project(u, shift, scale, u_pos=None):
        zx, dts = _in_proj(u, u_pos, nw, shift, scale, p["in_proj"], p["layer"], dt_bias)
        xbc = _conv_silu(_unflat(zx, b), p["conv_w"], p["conv_b"], d_inner)
        dtc5, dtr5 = _dt_layouts(_unflat(dts, b), hpg)
        return zx, xbc, dtc5, dtr5

    zx_c, xbc_c, dtc_c, dtr_c = project(_flat(ctx), ch1, cs1)
    zx_l, xbc_l, dtc_l, dtr_l = project(x, sh1, sc1, pos)
    h0 = jnp.zeros((b, 2, SSD_GROUPS, SSD_STATE, gw), F32)
    y_c, st_c = _ssd_scan(xbc_c, dtc_c, dtr_c, alog, skip_diag, h0, d_inner)
    y_l, _ = _ssd_scan(xbc_l, dtc_l, dtr_l, alog, skip_diag, st_c, d_inner)
    x = _ssd_out(y_l, zx_l, p["norm_w"], p["out_proj"], p["layer"], x, g1, pos)
    if with_ctx:
        y_flat = y_c.reshape(2, 1, -1, d_inner)
        ctx = _unflat(_ssd_out(y_flat, zx_c, p["norm_w"], p["out_proj"], p["layer"], _flat(ctx), cg1), b)
    return x, ctx


def _fourier_layer(x, ctx, mods, mods_c, nw, w_out, with_ctx):
    d = x.shape[-1]
    gw = d // N_FOURIER_GROUPS
    cc, sc_ = _dft_tables(gw, gw ** -0.5)
    w_cs = jnp.concatenate([cc, sc_], axis=1).astype(BF16)

    def mix(u, shift, scale, gate):
        l = u.shape[1]
        cl, sl = _dft_tables(l, l ** -0.5)
        a_mat = jnp.concatenate([cl, -sl], axis=1).astype(BF16)
        pq = _chan_dft(u, nw, shift, scale, w_cs)
        f = _seq_dft(a_mat, pq.reshape(u.shape[0], 2 * l, d))
        if gate.shape[0] == 1:
            return _unflat(_matmul_residual(_flat(f), w_out, _flat(u), gate), u.shape[0])
        return _matmul_residual(f, w_out, u, gate)

    x = mix(x, mods[0], mods[1], mods[2])
    if with_ctx:
        b = ctx.shape[0]
        ctx = mix(ctx, jnp.broadcast_to(mods_c[0], (b, 1, d)), jnp.broadcast_to(mods_c[1], (b, 1, d)), mods_c[2])
    return x, ctx


def kernel(x, c, ctx, c_ctx, w_mod, b_mod, norm_w, ffn_w_gate, ffn_w_up, ffn_w_down, ssd_in_proj, ssd_conv_w,
           ssd_conv_b, ssd_dt_bias, ssd_a_log, ssd_d, ssd_norm_w, ssd_out_proj, fourier_w_out, pool_w, pool_scale,
           final_norm_w):
    b, l, d = x.shape
    depth = w_mod.shape[0]
    n_ctx_rows = V7X_SUBLANES
    c_all = jnp.concatenate([c, jnp.broadcast_to(c_ctx[None, :], (n_ctx_rows, d))], axis=0)
    mod_all = _modulation(c_all, w_mod, b_mod)

    wg16, wu16, wd16 = ffn_w_gate.astype(BF16), ffn_w_up.astype(BF16), ffn_w_down.astype(BF16)
    in_proj16, out_proj16 = ssd_in_proj.astype(BF16), ssd_out_proj.astype(BF16)

    pos = _pos_tables(l, d)
    for i in range(depth):
        kind, j = i % N_MIXERS, i // N_MIXERS
        last = i == depth - 1
        mods = [mod_all[i, :b, None, k * d:(k + 1) * d] for k in range(6)]
        mods_c = [mod_all[i, b:b + 1, None, k * d:(k + 1) * d] for k in range(6)]
        nw1 = norm_w[i, 0][None, :]
        nw2 = norm_w[i, 1][None, :]
        if kind == 0:
            p = dict(in_proj=in_proj16, out_proj=out_proj16, layer=j, conv_w=ssd_conv_w[j],
                     conv_b=ssd_conv_b[j][None, :], dt_bias=ssd_dt_bias[j], a_log=ssd_a_log[j], d=ssd_d[j],
                     norm_w=ssd_norm_w[j][None, :])
            x, ctx = _ssd_layer(x, ctx, mods, mods_c, nw1, p, not last, pos if i == 0 else None)
        elif kind == 1:
            x, ctx = _fourier_layer(x, ctx, mods, mods_c, nw1, fourier_w_out[j].astype(BF16), not last)
        else:
            pw = pool_w[j].astype(BF16)
            ps = pool_scale[j][None, :]
            x = _pool_mixer(x, nw1, mods[0], mods[1], mods[2], pw, ps)
            if not last:
                ctx = _pool_mixer(ctx, nw1, *[jnp.broadcast_to(m, (b, 1, d)) for m in mods_c[:3]], pw, ps)
        fw = final_norm_w[None, :] if last else None
        x = _ffn(x, nw2, mods[3], mods[4], mods[5], wg16, wu16, wd16, i, fw)
        if not last:
            ctx = _unflat(_ffn(_flat(ctx), nw2, mods_c[3], mods_c[4], mods_c[5], wg16, wu16, wd16, i), b)
    return x
```
